```python
import math
import jax, jax.numpy as jnp
from jax import lax
import numpy as np

D_MODEL = 2048
BATCH = 1
SEQ = 8192
DEPTH = 4

GRID_W = 64
CTX_LEN = 256
MLA_HEADS = 8
MLA_Q_RANK = 512
MLA_KV_RANK = 512
MLA_NOPE = 128
MLA_ROPE = 64
MLA_V = 128
MLA_WIDTH = MLA_HEADS * MLA_V
DIFF_HEADS = 4
DIFF_QK = 128
DIFF_V = 2 * DIFF_QK
DIFF_WIDTH = DIFF_HEADS * DIFF_V
D_FF = 5632
N_EXPERTS = 8
TOP_K = 2
D_FF_EXPERT = 2816
N_DENSE = (DEPTH + 1) // 2
N_MOE = DEPTH // 2
ROPE_BASE = 10000.0
NORM_EPS = 1e-6
Q_BLOCK = 128
IN_SPLITS = (MLA_Q_RANK, MLA_KV_RANK, MLA_ROPE, DIFF_HEADS * 2 * DIFF_QK, DIFF_HEADS * 2 * DIFF_QK,
             DIFF_WIDTH, D_MODEL, D_MODEL)
D_IN = sum(IN_SPLITS)
IN_OFFSETS = tuple(int(v) for v in np.cumsum(IN_SPLITS)[:-1])

kernel_name = "hybrid_mla_diffattn_moe_dit"


def rms_norm(x, gain):
    xf = x.astype(jnp.float32)
    y = xf * lax.rsqrt(jnp.mean(xf * xf, axis=-1, keepdims=True) + NORM_EPS)
    return (y * gain.astype(jnp.float32)).astype(x.dtype)


def rope_1d(x, pos):
    half = x.shape[-1] // 2
    freqs = ROPE_BASE ** (-jnp.arange(half, dtype=jnp.float32) / half)
    ang = pos.astype(jnp.float32)[:, None] * freqs
    cos, sin = jnp.cos(ang), jnp.sin(ang)
    x1, x2 = x[..., :half], x[..., half:]
    return jnp.concatenate([x1 * cos - x2 * sin, x2 * cos + x1 * sin], axis=-1).astype(x.dtype)


def axial_rope(x, row, col):
    d = x.shape[-1] // 2
    return jnp.concatenate([rope_1d(x[..., :d], row), rope_1d(x[..., d:], col)], axis=-1)


def rope_latent_only(x, n_ctx, row, col):
    return jnp.concatenate([x[..., :n_ctx, :], axial_rope(x[..., n_ctx:, :], row, col)], axis=-2)


def attention(q, k, v, scale):
    b, h, sq, dk = q.shape
    nb = sq // Q_BLOCK
    qb = jnp.moveaxis(q.reshape(b, h, nb, Q_BLOCK, dk), 2, 0)

    def one_block(qi):
        s = jnp.einsum('bhqd,bhkd->bhqk', qi, k).astype(jnp.float32) * scale
        p = jax.nn.softmax(s, axis=-1)
        return jnp.einsum('bhqk,bhkd->bhqd', p.astype(v.dtype), v)

    out = lax.map(one_block, qb)
    return jnp.moveaxis(out, 0, 2).reshape(b, h, sq, v.shape[-1])


def heads_to_channels(o):
    b, h, s, d = o.shape
    return o.transpose(0, 2, 1, 3).reshape(b, s, h * d)


def mixer(h, n_ctx, row, col, layer, need_ctx, w_in, w_uq, g_qn, w_ukv, g_kvn, lam, g_sub,
          w_po_mla, w_po_diff, w_out):
    b, n, _ = h.shape
    proj = h @ w_in
    cq, ckv, krope, dq, dk, dv, ga, gb = jnp.split(proj, IN_OFFSETS, axis=-1)

    q = (rms_norm(cq, g_qn) @ w_uq).reshape(b, n, MLA_HEADS, MLA_NOPE + MLA_ROPE).transpose(0, 2, 1, 3)
    kv = (rms_norm(ckv, g_kvn) @ w_ukv).reshape(b, n, MLA_HEADS, MLA_NOPE + MLA_V).transpose(0, 2, 1, 3)
    q_mla = jnp.concatenate([q[..., :MLA_NOPE], rope_latent_only(q[..., MLA_NOPE:], n_ctx, row, col)], axis=-1)
    k_rope = rope_latent_only(krope, n_ctx, row, col)
    k_mla = jnp.concatenate([kv[..., :MLA_NOPE],
                             jnp.broadcast_to(k_rope[:, None], (b, MLA_HEADS, n, MLA_ROPE))], axis=-1)
    v_mla = kv[..., MLA_NOPE:]

    dq = dq.reshape(b, n, DIFF_HEADS, 2, DIFF_QK).transpose(0, 2, 1, 3, 4)
    dk = dk.reshape(b, n, DIFF_HEADS, 2, DIFF_QK).transpose(0, 2, 1, 3, 4)
    v_d = dv.reshape(b, n, DIFF_HEADS, DIFF_V).transpose(0, 2, 1, 3)
    q1 = rope_latent_only(dq[..., 0, :], n_ctx, row, col)
    q2 = rope_latent_only(dq[..., 1, :], n_ctx, row, col)
    k1 = rope_latent_only(dk[..., 0, :], n_ctx, row, col)
    k2 = rope_latent_only(dk[..., 1, :], n_ctx, row, col)
    lam_init = 0.8 - 0.6 * math.exp(-0.3 * layer)
    lf = lam.astype(jnp.float32)
    lam_full = jnp.exp(jnp.sum(lf[0] * lf[1])) - jnp.exp(jnp.sum(lf[2] * lf[3])) + lam_init

    s_mla = (MLA_NOPE + MLA_ROPE) ** -0.5
    s_diff = DIFF_QK ** -0.5

    def branches(qs, ks):
        o_mla = attention(q_mla[:, :, qs], k_mla[:, :, ks], v_mla[:, :, ks], s_mla)
        o1 = attention(q1[:, :, qs], k1[:, :, ks], v_d[:, :, ks], s_diff)
        o2 = attention(q2[:, :, qs], k2[:, :, ks], v_d[:, :, ks], s_diff)
        o_d = rms_norm(o1 - lam_full.astype(o1.dtype) * o2, g_sub) * (1.0 - lam_init)
        y = (jax.nn.sigmoid(ga[:, qs]) * (heads_to_channels(o_mla) @ w_po_mla)
             + jax.nn.sigmoid(gb[:, qs]) * (heads_to_channels(o_d) @ w_po_diff))
        return y @ w_out

    y_lat = branches(slice(n_ctx, None), slice(None))
    y_ctx = branches(slice(0, n_ctx), slice(0, n_ctx)) if need_ctx else None
    return y_ctx, y_lat


def swiglu(h, w1, w3, w2):
    return (jax.nn.silu(h @ w1) * (h @ w3)) @ w2


def moe_swiglu(h, w_router, w1, w3, w2):
    logits = (h @ w_router).astype(jnp.float32)
    top_vals, top_idx = lax.top_k(logits, TOP_K)
    top_w = jax.nn.softmax(top_vals, axis=-1)
    gates = jnp.sum(jax.nn.one_hot(top_idx, N_EXPERTS, dtype=jnp.float32) * top_w[..., None],
                    axis=-2).astype(h.dtype)
    out = jnp.zeros_like(h)
    for e in range(N_EXPERTS):
        out = out + gates[..., e:e + 1] * swiglu(h, w1[e], w3[e], w2[e])
    return out


def setup_inputs(seed: int = 0) -> dict:
    key = jax.random.key(seed)
    ks = jax.random.split(key, 24)
    f32 = jnp.float32

    def nrm(k, shape, fan_in):
        return jax.random.normal(k, shape, f32) * fan_in ** -0.5

    def gain(k, shape):
        return 1.0 + 0.02 * jax.random.normal(k, shape, f32)

    D = D_MODEL
    return {
        "x": jax.random.normal(ks[0], (BATCH, SEQ, D), f32),
        "c": jax.random.normal(ks[1], (BATCH, D), f32),
        "ctx": jax.random.normal(ks[2], (BATCH, CTX_LEN, D), f32),
        "c_ctx": jax.random.normal(ks[3], (D,), f32),
        "w_mod": nrm(ks[4], (DEPTH, D, 6 * D), D),
        "b_mod": 0.02 * jax.random.normal(ks[5], (DEPTH, 6 * D), f32),
        "g_norm": gain(ks[6], (DEPTH, 4, D)),
        "w_in": nrm(ks[7], (DEPTH, D, D_IN), D),
        "w_uq": nrm(ks[8], (DEPTH, MLA_Q_RANK, MLA_HEADS * (MLA_NOPE + MLA_ROPE)), MLA_Q_RANK),
        "g_qn": gain(ks[9], (DEPTH, MLA_Q_RANK)),
        "w_ukv": nrm(ks[10], (DEPTH, MLA_KV_RANK, MLA_HEADS * (MLA_NOPE + MLA_V)), MLA_KV_RANK),
        "g_kvn": gain(ks[11], (DEPTH, MLA_KV_RANK)),
        "lam": 0.1 * jax.random.normal(ks[12], (DEPTH, 4, DIFF_QK), f32),
        "g_sub": gain(ks[13], (DEPTH, DIFF_V)),
        "w_po_mla": nrm(ks[14], (DEPTH, MLA_WIDTH, D), MLA_WIDTH),
        "w_po_diff": nrm(ks[15], (DEPTH, DIFF_WIDTH, D), DIFF_WIDTH),
        "w_out": nrm(ks[16], (DEPTH, D, D), D),
        "w1_dense": nrm(ks[17], (N_DENSE, D, D_FF), D),
        "w3_dense": nrm(ks[18], (N_DENSE, D, D_FF), D),
        "w2_dense": nrm(ks[19], (N_DENSE, D_FF, D), D_FF),
        "w_router": nrm(ks[20], (N_MOE, D, N_EXPERTS), D),
        "w1_moe": nrm(ks[21], (N_MOE, N_EXPERTS, D, D_FF_EXPERT), D),
        "w3_moe": nrm(ks[22], (N_MOE, N_EXPERTS, D, D_FF_EXPERT), D),
        "w2_moe": nrm(ks[23], (N_MOE, N_EXPERTS, D_FF_EXPERT, D), D_FF_EXPERT),
    }


def reference(x, c, ctx, c_ctx, w_mod, b_mod, g_norm, w_in, w_uq, g_qn, w_ukv, g_kvn, lam, g_sub,
              w_po_mla, w_po_diff, w_out, w1_dense, w3_dense, w2_dense, w_router, w1_moe, w3_moe, w2_moe):
    b, s, d = x.shape
    n_ctx = ctx.shape[1]
    rows = s // GRID_W
    row = jnp.repeat(jnp.arange(rows, dtype=jnp.int32), GRID_W)
    col = jnp.tile(jnp.arange(GRID_W, dtype=jnp.int32), rows)
    sc = jax.nn.silu(c)
    scc = jax.nn.silu(c_ctx)
    for layer in range(DEPTH):
        need_ctx = layer < DEPTH - 1
        mod_l = (sc @ w_mod[layer] + b_mod[layer])[:, None, :]
        mod_c = scc @ w_mod[layer] + b_mod[layer]
        sh1, sc1, gt1, sh2, sc2, gt2 = jnp.split(mod_l, 6, axis=-1)
        csh1, csc1, cgt1, csh2, csc2, cgt2 = jnp.split(mod_c, 6, axis=-1)
        gn = g_norm[layer]

        h = jnp.concatenate([rms_norm(ctx, gn[0]) * (1.0 + csc1) + csh1,
                             rms_norm(x, gn[0]) * (1.0 + sc1) + sh1], axis=1)
        y_ctx, y_lat = mixer(h, n_ctx, row, col, layer, need_ctx, w_in[layer], w_uq[layer], g_qn[layer],
                             w_ukv[layer], g_kvn[layer], lam[layer], g_sub[layer], w_po_mla[layer],
                             w_po_diff[layer], w_out[layer])
        x = x + gt1 * rms_norm(y_lat, gn[1])
        if need_ctx:
            ctx = ctx + cgt1 * rms_norm(y_ctx, gn[1])

        h_lat = rms_norm(x, gn[2]) * (1.0 + sc2) + sh2
        if need_ctx:
            h = jnp.concatenate([rms_norm(ctx, gn[2]) * (1.0 + csc2) + csh2, h_lat], axis=1)
        else:
            h = h_lat
        if layer % 2 == 0:
            i = layer // 2
            f = swiglu(h, w1_dense[i], w3_dense[i], w2_dense[i])
        else:
            i = layer // 2
            f = moe_swiglu(h, w_router[i], w1_moe[i], w3_moe[i], w2_moe[i])
        x = x + gt2 * rms_norm(f[:, -s:], gn[3])
        if need_ctx:
            ctx = ctx + cgt2 * rms_norm(f[:, :n_ctx], gn[3])
    return x
```

```python
import functools
import math

import jax
import jax.numpy as jnp
from jax import lax
from jax.experimental import pallas as pl
from jax.experimental.pallas import tpu as pltpu

F32 = jnp.float32
BF16 = jnp.bfloat16

D_MODEL = 2048
GRID_W = 64
MLA_HEADS = 8
MLA_Q_RANK = 512
MLA_KV_RANK = 512
MLA_NOPE = 128
MLA_ROPE = 64
MLA_V = 128
MLA_QK_PAD = 256
DIFF_HEADS = 4
DIFF_QK = 128
DIFF_V = 256
N_EXPERTS = 8
ROPE_BASE = 10000.0
NORM_EPS = 1e-6
LANES = 128

_OFF_CQ, _OFF_CKV, _OFF_DQ, _OFF_DK, _OFF_DV, _OFF_GA, _OFF_GB, _OFF_KR = (
    0, 512, 1024, 2048, 3072, 4096, 6144, 8192)
D_IN_PAD = 8448

VMEM_LIMIT = 56 * 1024 * 1024


def _pick(n, cands):
    for c in cands:
        if n % c == 0:
            return c
    raise ValueError(f"no tile for {n} in {cands}")


def _cparams(sem):
    return pltpu.CompilerParams(dimension_semantics=sem, vmem_limit_bytes=VMEM_LIMIT)


def _rms(xf, g):
    ms = jnp.mean(xf * xf, axis=-1, keepdims=True)
    return xf * lax.rsqrt(ms + NORM_EPS) * g


def _sigmoid(v):
    return 1.0 / (1.0 + jnp.exp(-v))


def _row_groups(i, tm, n_rows, n_lat, fn):
    if n_rows <= n_lat:
        fn(0, tm, False)
        return
    nb = n_rows // tm
    b, off = divmod(n_lat, tm)
    if b > 0:
        pl.when(i < b)(lambda: fn(0, tm, False))
    if off:
        def _split():
            fn(0, off, False)
            fn(off, tm, True)
        pl.when(i == b)(_split)
        if nb > b + 1:
            pl.when(i > b)(lambda: fn(0, tm, True))
    else:
        pl.when(i >= b)(lambda: fn(0, tm, True))


def _mod_kernel(c_ref, w_ref, b_ref, o_ref):
    cv = c_ref[...]
    s = cv * _sigmoid(cv)
    o_ref[0] = jnp.dot(s, w_ref[0], preferred_element_type=F32,
                       precision=lax.Precision.HIGHEST) + b_ref[0]


def _mod_call(c8, w_mod, b_mod):
    depth, d, n6 = w_mod.shape
    tn = _pick(n6, (1536, 1024, 512, 256, 128))
    return pl.pallas_call(
        _mod_kernel,
        grid=(depth, n6 // tn),
        in_specs=[pl.BlockSpec((8, d), lambda l, j: (0, 0)),
                  pl.BlockSpec((1, d, tn), lambda l, j: (l, 0, j)),
                  pl.BlockSpec((1, 1, tn), lambda l, j: (l, 0, j))],
        out_specs=pl.BlockSpec((1, 8, tn), lambda l, j: (l, 0, j)),
        out_shape=jax.ShapeDtypeStruct((depth, 8, n6), F32),
        compiler_params=_cparams(("arbitrary", "arbitrary")),
        name="adaln_mod",
    )(c8, w_mod, b_mod.reshape(depth, 1, n6))


def _norm_matmul_kernel(x_ref, g_ref, mod_ref, w_ref, o_ref, h_ref, *, tm, n_rows, n_lat):
    i = pl.program_id(0)
    j = pl.program_id(1)

    @pl.when(j == 0)
    def _():
        def fill(r0, r1, is_ctx):
            r = 2 if is_ctx else 0
            h = _rms(x_ref[r0:r1, :], g_ref[...]) * (1.0 + mod_ref[r:r + 1, :]) + mod_ref[r + 1:r + 2, :]
            h_ref[r0:r1, :] = h.astype(BF16)
        _row_groups(i, tm, n_rows, n_lat, fill)

    o_ref[...] = jnp.dot(h_ref[...], w_ref[...], preferred_element_type=F32).astype(o_ref.dtype)


def _norm_matmul_call(xs, g, mod, w, n_lat):
    n_rows, d = xs.shape
    n_out = w.shape[1]
    tm = _pick(n_rows, (768, 512, 256))
    tn = _pick(n_out, (1408, 1024, 768, 512, 256))
    kern = functools.partial(_norm_matmul_kernel, tm=tm, n_rows=n_rows, n_lat=n_lat)
    return pl.pallas_call(
        kern,
        grid=(n_rows // tm, n_out // tn),
        in_specs=[pl.BlockSpec((tm, d), lambda i, j: (i, 0)),
                  pl.BlockSpec((1, d), lambda i, j: (0, 0)),
                  pl.BlockSpec((8, d), lambda i, j: (0, 0)),
                  pl.BlockSpec((d, tn), lambda i, j: (0, j))],
        out_specs=pl.BlockSpec((tm, tn), lambda i, j: (i, j)),
        out_shape=jax.ShapeDtypeStruct((n_rows, n_out), BF16),
        scratch_shapes=[pltpu.VMEM((tm, d), BF16)],
        compiler_params=_cparams(("arbitrary", "arbitrary")),
        name="norm_in_proj",
    )(xs, g.reshape(1, d), mod, w)


def _rope(xf, cos, sin):
    return xf * cos + pltpu.roll(xf, LANES // 2, 1) * sin


def _prep_kernel(cq_ref, ckv_ref, dq_ref, dk_ref, kr_ref, gq_ref, gkv_ref, wuq_ref, wk_ref, wv_ref,
                 cosd_ref, sind_ref, cosm_ref, sinm_ref,
                 qm_ref, km_ref, vm_ref, qd_ref, kd_ref, *, s_mla, s_diff):
    cosm = cosm_ref[...]
    sinm = sinm_ref[...]
    cosd = cosd_ref[...]
    sind = sind_ref[...]

    cqn = _rms(cq_ref[...].astype(F32), gq_ref[...]).astype(BF16)
    q = jnp.dot(cqn, wuq_ref[...], preferred_element_type=F32)
    for h in range(MLA_HEADS):
        base = h * MLA_QK_PAD
        qm_ref[h, :, 0:LANES] = (q[:, base:base + LANES] * s_mla).astype(BF16)
        qr = _rope(q[:, base + LANES:base + 2 * LANES], cosm, sinm)
        qm_ref[h, :, LANES:2 * LANES] = (qr * s_mla).astype(BF16)

    ckvn = _rms(ckv_ref[...].astype(F32), gkv_ref[...]).astype(BF16)
    kn = jnp.dot(ckvn, wk_ref[...], preferred_element_type=F32)
    vv = jnp.dot(ckvn, wv_ref[...], preferred_element_type=F32)
    kr = _rope(kr_ref[...].astype(F32), cosm, sinm).astype(BF16)
    for h in range(MLA_HEADS):
        km_ref[h, :, 0:LANES] = kn[:, h * LANES:(h + 1) * LANES].astype(BF16)
        km_ref[h, :, LANES:2 * LANES] = kr
        vm_ref[h] = vv[:, h * LANES:(h + 1) * LANES].astype(BF16)

    for hc in range(2 * DIFF_HEADS):
        xq = dq_ref[:, hc * LANES:(hc + 1) * LANES].astype(F32)
        qd_ref[hc] = (_rope(xq, cosd, sind) * s_diff).astype(BF16)
        xk = dk_ref[:, hc * LANES:(hc + 1) * LANES].astype(F32)
        kd_ref[hc] = _rope(xk, cosd, sind).astype(BF16)


def _prep_call(proj, gq, gkv, wuq, wk, wv, cosd, sind, cosm, sinm):
    n = proj.shape[0]
    tm = _pick(n, (384, 256, 128))
    nh, nd = MLA_HEADS, 2 * DIFF_HEADS
    kern = functools.partial(_prep_kernel, s_mla=float((MLA_NOPE + MLA_ROPE) ** -0.5),
                             s_diff=float(DIFF_QK ** -0.5))
    row = lambda w, c: pl.BlockSpec((tm, w), lambda i, c=c: (i, c))
    const = lambda a: pl.BlockSpec(a.shape, lambda i: (0,) * a.ndim)
    tab = pl.BlockSpec((tm, LANES), lambda i: (i, 0))
    return pl.pallas_call(
        kern,
        grid=(n // tm,),
        in_specs=[row(512, _OFF_CQ // 512), row(512, _OFF_CKV // 512), row(1024, _OFF_DQ // 1024),
                  row(1024, _OFF_DK // 1024), row(LANES, _OFF_KR // LANES),
                  const(gq), const(gkv), const(wuq), const(wk), const(wv), tab, tab, tab, tab],
        out_specs=[pl.BlockSpec((nh, tm, MLA_QK_PAD), lambda i: (0, i, 0)),
                   pl.BlockSpec((nh, tm, MLA_QK_PAD), lambda i: (0, i, 0)),
                   pl.BlockSpec((nh, tm, MLA_V), lambda i: (0, i, 0)),
                   pl.BlockSpec((nd, tm, DIFF_QK), lambda i: (0, i, 0)),
                   pl.BlockSpec((nd, tm, DIFF_QK), lambda i: (0, i, 0))],
        out_shape=[jax.ShapeDtypeStruct((nh, n, MLA_QK_PAD), BF16),
                   jax.ShapeDtypeStruct((nh, n, MLA_QK_PAD), BF16),
                   jax.ShapeDtypeStruct((nh, n, MLA_V), BF16),
                   jax.ShapeDtypeStruct((nd, n, DIFF_QK), BF16),
                   jax.ShapeDtypeStruct((nd, n, DIFF_QK), BF16)],
        compiler_params=_cparams(("arbitrary",)),
        name="qkv_prep",
    )(proj, proj, proj, proj, proj, gq, gkv, wuq, wk, wv, cosd, sind, cosm, sinm)


def _softmax_pv(q, k_at, v_at, tk, n_chunks, dv):
    tq = q.shape[0]

    def step(off, carry):
        m, l, acc = carry
        s = lax.dot_general(q, k_at(off), (((1,), (1,)), ((), ())), preferred_element_type=F32)
        m_new = jnp.maximum(m, jnp.max(s, axis=-1, keepdims=True))
        alpha = jnp.exp(m - m_new)
        p = jnp.exp(s - m_new)
        l = alpha * l + jnp.sum(p, axis=-1, keepdims=True)
        acc = alpha * acc + jnp.dot(p.astype(BF16), v_at(off), preferred_element_type=F32)
        return m_new, l, acc

    init = (jnp.full((tq, 1), -jnp.inf, F32), jnp.zeros((tq, 1), F32), jnp.zeros((tq, dv), F32))
    if n_chunks == 1:
        m, l, acc = step(0, init)
    else:
        m, l, acc = lax.fori_loop(0, n_chunks, lambda c, carry: step(pl.multiple_of(c * tk, tk), carry), init)
    return acc / l


def _mla_attn_kernel(q_ref, k_ref, v_ref, *rest, tk, n_chunks):
    o_ref = rest[-1]
    o = _softmax_pv(q_ref[0],
                    lambda off: k_ref[0, pl.ds(off, tk), :],
                    lambda off: v_ref[0, pl.ds(off, tk), :],
                    tk, n_chunks, MLA_V)
    o_ref[...] = o.astype(o_ref.dtype)


def _mla_attn_call(qm, km, vm, q_row0, n_q, kv_row0, n_kv, n_tok, prev=None):
    tq = _pick(n_q, (512, 256, 128))
    tk = _pick(n_kv, (768, 512, 256, 128))
    assert q_row0 % tq == 0 and kv_row0 % n_kv == 0
    qb, kb = q_row0 // tq, kv_row0 // n_kv
    kern = functools.partial(_mla_attn_kernel, tk=tk, n_chunks=n_kv // tk)
    in_specs = [pl.BlockSpec((1, tq, MLA_QK_PAD), lambda h, i: (h, i + qb, 0)),
                pl.BlockSpec((1, n_kv, MLA_QK_PAD), lambda h, i: (h, kb, 0)),
                pl.BlockSpec((1, n_kv, MLA_V), lambda h, i: (h, kb, 0))]
    args = [qm, km, vm]
    aliases = {}
    if prev is not None:
        in_specs.append(pl.BlockSpec(memory_space=pl.ANY))
        args.append(prev)
        aliases = {3: 0}
    return pl.pallas_call(
        kern,
        grid=(MLA_HEADS, n_q // tq),
        in_specs=in_specs,
        out_specs=pl.BlockSpec((tq, MLA_V), lambda h, i: (i + qb, h)),
        out_shape=jax.ShapeDtypeStruct((n_tok, MLA_HEADS * MLA_V), BF16),
        input_output_aliases=aliases,
        compiler_params=_cparams(("arbitrary", "arbitrary")),
        name="mla_attention",
    )(*args)


def _diff_attn_kernel(q_ref, k_ref, v_ref, lam_ref, gs_ref, *rest, tk, n_chunks, lam_init):
    o_ref = rest[-1]
    outs = []
    for comp in range(2):
        outs.append(_softmax_pv(q_ref[comp],
                                lambda off, comp=comp: k_ref[comp, pl.ds(off, tk), :],
                                lambda off: v_ref[pl.ds(off, tk), :],
                                tk, n_chunks, DIFF_V))
    lf = lam_ref[...]
    lam_full = (jnp.exp(jnp.sum(lf[0:1] * lf[1:2], axis=-1, keepdims=True))
                - jnp.exp(jnp.sum(lf[2:3] * lf[3:4], axis=-1, keepdims=True)) + lam_init)
    d = outs[0] - lam_full * outs[1]
    o_ref[...] = (_rms(d, gs_ref[...]) * (1.0 - lam_init)).astype(o_ref.dtype)


def _diff_attn_call(qd, kd, proj, lam, gsub, lam_init, q_row0, n_q, kv_row0, n_kv, n_tok, prev=None):
    tq = _pick(n_q, (512, 256, 128))
    tk = _pick(n_kv, (768, 512, 256, 128))
    assert q_row0 % tq == 0 and kv_row0 % n_kv == 0
    qb, kb = q_row0 // tq, kv_row0 // n_kv
    vcol = _OFF_DV // DIFF_V
    kern = functools.partial(_diff_attn_kernel, tk=tk, n_chunks=n_kv // tk, lam_init=float(lam_init))
    in_specs = [pl.BlockSpec((2, tq, DIFF_QK), lambda h, i: (h, i + qb, 0)),
                pl.BlockSpec((2, n_kv, DIFF_QK), lambda h, i: (h, kb, 0)),
                pl.BlockSpec((n_kv, DIFF_V), lambda h, i: (kb, vcol + h)),
                pl.BlockSpec((4, DIFF_QK), lambda h, i: (0, 0)),
                pl.BlockSpec((1, DIFF_V), lambda h, i: (0, 0))]
    args = [qd, kd, proj, lam, gsub.reshape(1, DIFF_V)]
    aliases = {}
    if prev is not None:
        in_specs.append(pl.BlockSpec(memory_space=pl.ANY))
        args.append(prev)
        aliases = {5: 0}
    return pl.pallas_call(
        kern,
        grid=(DIFF_HEADS, n_q // tq),
        in_specs=in_specs,
        out_specs=pl.BlockSpec((tq, DIFF_V), lambda h, i: (i + qb, h)),
        out_shape=jax.ShapeDtypeStruct((n_tok, DIFF_HEADS * DIFF_V), BF16),
        input_output_aliases=aliases,
        compiler_params=_cparams(("arbitrary", "arbitrary")),
        name="diff_attention",
    )(*args)


def _post_kernel(om_ref, od_ref, ga_ref, gb_ref, x_ref, wpm_ref, wpd_ref, wo_ref, g_ref, gate_ref, o_ref,
                 *, tm, n_rows, n_lat):
    i = pl.program_id(0)
    a = jnp.dot(om_ref[...], wpm_ref[...], preferred_element_type=F32)
    b = jnp.dot(od_ref[...], wpd_ref[...], preferred_element_type=F32)
    y = _sigmoid(ga_ref[...].astype(F32)) * a + _sigmoid(gb_ref[...].astype(F32)) * b
    z = jnp.dot(y.astype(BF16), wo_ref[...], preferred_element_type=F32)
    zn = _rms(z, g_ref[...])
    if n_rows > n_lat:
        gate = jnp.where(i * tm >= n_lat, gate_ref[1:2, :], gate_ref[0:1, :])
    else:
        gate = gate_ref[0:1, :]
    o_ref[...] = x_ref[...] + gate * zn


def _post_call(o_mla, o_d, proj, xs, wpm, wpd, wo, g, gate, n_rows, n_lat):
    d = xs.shape[1]
    tm = 256
    assert n_rows % tm == 0 and n_lat % tm == 0
    kern = functools.partial(_post_kernel, tm=tm, n_rows=n_rows, n_lat=n_lat)
    const = lambda a: pl.BlockSpec(a.shape, lambda i: (0,) * a.ndim, pipeline_mode=pl.Buffered(1))
    return pl.pallas_call(
        kern,
        grid=(n_rows // tm,),
        in_specs=[pl.BlockSpec((tm, o_mla.shape[1]), lambda i: (i, 0)),
                  pl.BlockSpec((tm, o_d.shape[1]), lambda i: (i, 0)),
                  pl.BlockSpec((tm, d), lambda i: (i, _OFF_GA // D_MODEL)),
                  pl.BlockSpec((tm, d), lambda i: (i, _OFF_GB // D_MODEL)),
                  pl.BlockSpec((tm, d), lambda i: (i, 0)),
                  const(wpm), const(wpd), const(wo),
                  pl.BlockSpec((1, d), lambda i: (0, 0)),
                  pl.BlockSpec((8, d), lambda i: (0, 0))],
        out_specs=pl.BlockSpec((tm, d), lambda i: (i, 0)),
        out_shape=jax.ShapeDtypeStruct((n_rows, d), F32),
        compiler_params=_cparams(("arbitrary",)),
        name="merge_out_proj",
    )(o_mla, o_d, proj, proj, xs, wpm, wpd, wo, g.reshape(1, d), gate)


def _router_kernel(x_ref, g_ref, mod_ref, wr_ref, o_ref, *, tm, n_rows, n_lat):
    i = pl.program_id(0)

    def fill(r0, r1, is_ctx):
        r = 3 if is_ctx else 0
        h = _rms(x_ref[r0:r1, :], g_ref[...]) * (1.0 + mod_ref[r:r + 1, :]) + mod_ref[r + 1:r + 2, :]
        logits = jnp.dot(h, wr_ref[...], preferred_element_type=F32, precision=lax.Precision.HIGHEST)
        lane = lax.broadcasted_iota(jnp.int32, logits.shape, 1).astype(F32)
        lg = jnp.where(lane < N_EXPERTS, logits, -jnp.inf)
        m1 = jnp.max(lg, axis=-1, keepdims=True)
        i1 = jnp.min(jnp.where(lg == m1, lane, float(LANES)), axis=-1, keepdims=True)
        lg2 = jnp.where(lane == i1, -jnp.inf, lg)
        m2 = jnp.max(lg2, axis=-1, keepdims=True)
        i2 = jnp.min(jnp.where(lg2 == m2, lane, float(LANES)), axis=-1, keepdims=True)
        e = jnp.exp(m2 - m1)
        den = 1.0 + e
        o_ref[r0:r1, :] = jnp.where(lane == i1, 1.0 / den, 0.0) + jnp.where(lane == i2, e / den, 0.0)

    _row_groups(i, tm, n_rows, n_lat, fill)


def _router_call(xs, g, mod, wr_pad, n_rows, n_lat):
    d = xs.shape[1]
    tm = _pick(n_rows, (768, 512, 256))
    kern = functools.partial(_router_kernel, tm=tm, n_rows=n_rows, n_lat=n_lat)
    return pl.pallas_call(
        kern,
        grid=(n_rows // tm,),
        in_specs=[pl.BlockSpec((tm, d), lambda i: (i, 0)),
                  pl.BlockSpec((1, d), lambda i: (0, 0)),
                  pl.BlockSpec((8, d), lambda i: (0, 0)),
                  pl.BlockSpec((d, LANES), lambda i: (0, 0))],
        out_specs=pl.BlockSpec((tm, LANES), lambda i: (i, 0)),
        out_shape=jax.ShapeDtypeStruct((n_rows, LANES), F32),
        compiler_params=_cparams(("arbitrary",)),
        name="moe_router",
    )(xs, g.reshape(1, d), mod, wr_pad)


def _ffn_kernel(*refs, tm, n_rows, n_lat, n_exp, n_k, gated):
    if gated:
        x_ref, g2_ref, mod_ref, w1_ref, w3_ref, w2_ref, g3_ref, gates_ref, o_ref, h_ref, acc_ref = refs
    else:
        x_ref, g2_ref, mod_ref, w1_ref, w3_ref, w2_ref, g3_ref, o_ref, h_ref, acc_ref = refs
    i = pl.program_id(0)
    e = pl.program_id(1)
    k = pl.program_id(2)

    @pl.when(jnp.logical_and(e == 0, k == 0))
    def _():
        def fill(r0, r1, is_ctx):
            r = 3 if is_ctx else 0
            h = _rms(x_ref[r0:r1, :], g2_ref[...]) * (1.0 + mod_ref[r:r + 1, :]) + mod_ref[r + 1:r + 2, :]
            h_ref[r0:r1, :] = h.astype(BF16)
        _row_groups(i, tm, n_rows, n_lat, fill)
        acc_ref[...] = jnp.zeros_like(acc_ref)

    h = h_ref[...]
    u1 = jnp.dot(h, w1_ref[0], preferred_element_type=F32)
    u3 = jnp.dot(h, w3_ref[0], preferred_element_type=F32)
    u = (u1 * _sigmoid(u1)) * u3
    if gated:
        gt = gates_ref[...]
        lane = lax.broadcasted_iota(jnp.int32, gt.shape, 1)
        u = u * jnp.sum(jnp.where(lane == e, gt, 0.0), axis=-1, keepdims=True)
    acc_ref[...] += jnp.dot(u.astype(BF16), w2_ref[0], preferred_element_type=F32)

    @pl.when(jnp.logical_and(e == n_exp - 1, k == n_k - 1))
    def _():
        def fin(r0, r1, is_ctx):
            r = 5 if is_ctx else 2
            fn = _rms(acc_ref[r0:r1, :], g3_ref[...])
            o_ref[r0:r1, :] = x_ref[r0:r1, :] + mod_ref[r:r + 1, :] * fn
        _row_groups(i, tm, n_rows, n_lat, fin)


def _ffn_call(xs, g2, mod, w1, w3, w2, g3, gates, n_rows, n_lat):
    d = xs.shape[1]
    n_exp, _, f = w1.shape
    tm = _pick(n_rows, (768, 512, 256))
    tf = _pick(f, (512, 256, 128))
    n_k = f // tf
    gated = gates is not None
    kern = functools.partial(_ffn_kernel, tm=tm, n_rows=n_rows, n_lat=n_lat, n_exp=n_exp, n_k=n_k, gated=gated)
    in_specs = [pl.BlockSpec((tm, d), lambda i, e, k: (i, 0)),
                pl.BlockSpec((1, d), lambda i, e, k: (0, 0)),
                pl.BlockSpec((8, d), lambda i, e, k: (0, 0)),
                pl.BlockSpec((1, d, tf), lambda i, e, k: (e, 0, k)),
                pl.BlockSpec((1, d, tf), lambda i, e, k: (e, 0, k)),
                pl.BlockSpec((1, tf, d), lambda i, e, k: (e, k, 0)),
                pl.BlockSpec((1, d), lambda i, e, k: (0, 0))]
    args = [xs, g2.reshape(1, d), mod, w1, w3, w2, g3.reshape(1, d)]
    if gated:
        in_specs.append(pl.BlockSpec((tm, LANES), lambda i, e, k: (i, 0)))
        args.append(gates)
    return pl.pallas_call(
        kern,
        grid=(n_rows // tm, n_exp, n_k),
        in_specs=in_specs,
        out_specs=pl.BlockSpec((tm, d), lambda i, e, k: (i, 0)),
        out_shape=jax.ShapeDtypeStruct((n_rows, d), F32),
        scratch_shapes=[pltpu.VMEM((tm, d), BF16), pltpu.VMEM((tm, d), F32)],
        compiler_params=_cparams(("arbitrary", "arbitrary", "arbitrary")),
        name="moe_ffn" if gated else "dense_ffn",
    )(*args)


def _diff_perm_cols(w, base):
    parts = []
    for b in range(8):
        o = base + b * LANES
        parts += [w[..., o:o + 32], w[..., o + 64:o + 96], w[..., o + 32:o + 64], w[..., o + 96:o + 128]]
    return parts


def _mla_rope_slot(w, base):
    z = jnp.zeros(w.shape[:-1] + (32,), w.dtype)
    return [w[..., base:base + 16], w[..., base + 32:base + 48], z,
            w[..., base + 16:base + 32], w[..., base + 48:base + 64], z]


def _prep_w_in(w_in):
    o_kr, o_dq, o_dk, o_dv, o_ga = 1024, 1088, 2112, 3136, 4160
    parts = [w_in[..., 0:1024]]
    parts += _diff_perm_cols(w_in, o_dq)
    parts += _diff_perm_cols(w_in, o_dk)
    parts += [w_in[..., o_dv:o_ga], w_in[..., o_ga:]]
    parts += _mla_rope_slot(w_in, o_kr)
    parts += [jnp.zeros(w_in.shape[:-1] + (LANES,), w_in.dtype)]
    return jnp.concatenate(parts, axis=-1).astype(BF16)


def _prep_w_uq(w_uq):
    parts = []
    per = MLA_NOPE + MLA_ROPE
    for h in range(MLA_HEADS):
        parts.append(w_uq[..., h * per:h * per + MLA_NOPE])
        parts += _mla_rope_slot(w_uq, h * per + MLA_NOPE)
    return jnp.concatenate(parts, axis=-1).astype(BF16)


def _prep_w_ukv(w_ukv):
    per = MLA_NOPE + MLA_V
    wk = jnp.concatenate([w_ukv[..., h * per:h * per + MLA_NOPE] for h in range(MLA_HEADS)], axis=-1)
    wv = jnp.concatenate([w_ukv[..., h * per + MLA_NOPE:(h + 1) * per] for h in range(MLA_HEADS)], axis=-1)
    return wk.astype(BF16), wv.astype(BF16)


def _rope_tables(n_lat, n_ctx):
    t = jnp.arange(n_lat, dtype=jnp.int32)
    row = (t // GRID_W).astype(F32)[:, None]
    col = (t % GRID_W).astype(F32)[:, None]

    def table(half, pad):
        freqs = ROPE_BASE ** (-jnp.arange(half, dtype=F32) / half)
        a = jnp.concatenate([row * freqs, col * freqs], axis=-1)
        z = jnp.zeros((n_lat, pad), F32)
        cos = jnp.concatenate([jnp.cos(a), 1.0 + z, jnp.cos(a), 1.0 + z], axis=-1)
        sin = jnp.concatenate([-jnp.sin(a), z, jnp.sin(a), z], axis=-1)
        cos = jnp.concatenate([cos, jnp.ones((n_ctx, LANES), F32)], axis=0)
        sin = jnp.concatenate([sin, jnp.zeros((n_ctx, LANES), F32)], axis=0)
        return cos, sin

    cosd, sind = table(DIFF_QK // 4, 0)
    cosm, sinm = table(MLA_ROPE // 4, 32)
    return cosd, sind, cosm, sinm


def kernel(x, c, ctx, c_ctx, w_mod, b_mod, g_norm, w_in, w_uq, g_qn, w_ukv, g_kvn, lam, g_sub, w_po_mla,
           w_po_diff, w_out, w1_dense, w3_dense, w2_dense, w_router, w1_moe, w3_moe, w2_moe):
    b, s, d = x.shape
    n_ctx = ctx.shape[1]
    depth = w_mod.shape[0]
    assert b == 1 and d == D_MODEL and s % GRID_W == 0
    n_tok = s + n_ctx

    xs = jnp.concatenate([x[0], ctx[0]], axis=0)
    c8 = jnp.concatenate([c, c_ctx[None, :], jnp.zeros((6, d), F32)], axis=0)
    mods = _mod_call(c8, w_mod, b_mod)

    w_in_p = _prep_w_in(w_in)
    w_uq_p = _prep_w_uq(w_uq)
    w_k_p, w_v_p = _prep_w_ukv(w_ukv)
    w_pm, w_pd, w_o = w_po_mla.astype(BF16), w_po_diff.astype(BF16), w_out.astype(BF16)
    w1d, w3d, w2d = w1_dense.astype(BF16), w3_dense.astype(BF16), w2_dense.astype(BF16)
    w1m, w3m, w2m = w1_moe.astype(BF16), w3_moe.astype(BF16), w2_moe.astype(BF16)
    wr_pad = jnp.pad(w_router, ((0, 0), (0, 0), (0, LANES - N_EXPERTS)))
    cosd, sind, cosm, sinm = _rope_tables(s, n_ctx)
    zrow = jnp.zeros((d,), F32)

    for layer in range(depth):
        need_ctx = layer < depth - 1
        lat, cx = mods[layer, 0], mods[layer, 1]
        seg = lambda v, k: v[k * d:(k + 1) * d]
        gn = g_norm[layer]

        mod1 = jnp.stack([seg(lat, 1), seg(lat, 0), seg(cx, 1), seg(cx, 0), zrow, zrow, zrow, zrow])
        proj = _norm_matmul_call(xs, gn[0], mod1, w_in_p[layer], s)
        qm, km, vm, qd, kd = _prep_call(proj, g_qn[layer].reshape(1, -1), g_kvn[layer].reshape(1, -1),
                                        w_uq_p[layer], w_k_p[layer], w_v_p[layer], cosd, sind, cosm, sinm)
        lam_init = 0.8 - 0.6 * math.exp(-0.3 * layer)
        o_mla = _mla_attn_call(qm, km, vm, 0, s, 0, n_tok, n_tok)
        o_d = _diff_attn_call(qd, kd, proj, lam[layer], g_sub[layer], lam_init, 0, s, 0, n_tok, n_tok)
        if need_ctx:
            o_mla = _mla_attn_call(qm, km, vm, s, n_ctx, s, n_ctx, n_tok, prev=o_mla)
            o_d = _diff_attn_call(qd, kd, proj, lam[layer], g_sub[layer], lam_init, s, n_ctx, s, n_ctx, n_tok,
                                  prev=o_d)
        n_rows = n_tok if need_ctx else s
        gate1 = jnp.stack([seg(lat, 2), seg(cx, 2), zrow, zrow, zrow, zrow, zrow, zrow])
        xs = _post_call(o_mla, o_d, proj, xs, w_pm[layer], w_pd[layer], w_o[layer], gn[1], gate1, n_rows, s)

        mod2 = jnp.stack([seg(lat, 4), seg(lat, 3), seg(lat, 5), seg(cx, 4), seg(cx, 3), seg(cx, 5), zrow, zrow])
        i = layer // 2
        if layer % 2 == 0:
            xs = _ffn_call(xs, gn[2], mod2, w1d[i:i + 1], w3d[i:i + 1], w2d[i:i + 1], gn[3], None, n_rows, s)
        else:
            gates = _router_call(xs, gn[2], mod2, wr_pad[i], n_rows, s)
            xs = _ffn_call(xs, gn[2], mod2, w1m[i], w3m[i], w2m[i], gn[3], gates, n_rows, s)
    return xs[:s][None]
```

```python
import functools
import math

import jax
import jax.numpy as jnp
from jax import lax
from jax.experimental import pallas as pl
from jax.experimental.pallas import tpu as pltpu

F32 = jnp.float32
BF16 = jnp.bfloat16

D_MODEL = 2048
GRID_W = 64
MLA_HEADS = 8
MLA_Q_RANK = 512
MLA_KV_RANK = 512
MLA_NOPE = 128
MLA_ROPE = 64
MLA_V = 128
MLA_QK_PAD = 256
DIFF_HEADS = 4
DIFF_QK = 128
DIFF_V = 256
N_EXPERTS = 8
ROPE_BASE = 10000.0
NORM_EPS = 1e-6
LANES = 128
LOG2E = math.log2(math.e)

_OFF_CQ, _OFF_CKV, _OFF_DQ, _OFF_DK, _OFF_DV, _OFF_GA, _OFF_GB, _OFF_KR = (
    0, 512, 1024, 2048, 3072, 4096, 6144, 8192)
D_IN_PAD = 8448

VMEM_LIMIT = 56 * 1024 * 1024


def _pick(n, cands):
    for c in cands:
        if n % c == 0:
            return c
    raise ValueError(f"no tile for {n} in {cands}")


def _cparams(sem):
    return pltpu.CompilerParams(dimension_semantics=sem, vmem_limit_bytes=VMEM_LIMIT)


def _rms(xf, g):
    ms = jnp.mean(xf * xf, axis=-1, keepdims=True)
    return xf * lax.rsqrt(ms + NORM_EPS) * g


def _sigmoid(v):
    return 1.0 / (1.0 + jnp.exp(-v))


def _row_groups(i, tm, n_rows, n_lat, fn):
    if n_rows <= n_lat:
        fn(0, tm, False)
        return
    nb = n_rows // tm
    b, off = divmod(n_lat, tm)
    if b > 0:
        pl.when(i < b)(lambda: fn(0, tm, False))
    if off:
        def _split():
            fn(0, off, False)
            fn(off, tm, True)
        pl.when(i == b)(_split)
        if nb > b + 1:
            pl.when(i > b)(lambda: fn(0, tm, True))
    else:
        pl.when(i >= b)(lambda: fn(0, tm, True))


def _mod_kernel(c_ref, w_ref, b_ref, o_ref):
    cv = c_ref[...]
    s = cv * _sigmoid(cv)
    o_ref[0] = jnp.dot(s, w_ref[0], preferred_element_type=F32,
                       precision=lax.Precision.HIGHEST) + b_ref[0]


def _mod_call(c8, w_mod, b_mod):
    depth, d, n6 = w_mod.shape
    tn = _pick(n6, (1536, 1024, 512, 256, 128))
    return pl.pallas_call(
        _mod_kernel,
        grid=(depth, n6 // tn),
        in_specs=[pl.BlockSpec((8, d), lambda l, j: (0, 0)),
                  pl.BlockSpec((1, d, tn), lambda l, j: (l, 0, j)),
                  pl.BlockSpec((1, 1, tn), lambda l, j: (l, 0, j))],
        out_specs=pl.BlockSpec((1, 8, tn), lambda l, j: (l, 0, j)),
        out_shape=jax.ShapeDtypeStruct((depth, 8, n6), F32),
        compiler_params=_cparams(("arbitrary", "arbitrary")),
        name="adaln_mod",
    )(c8, w_mod, b_mod.reshape(depth, 1, n6))


def _norm_matmul_kernel(x_ref, g_ref, mod_ref, w_ref, o_ref, h_ref, *, tm, n_rows, n_lat):
    i = pl.program_id(0)
    j = pl.program_id(1)

    @pl.when(j == 0)
    def _():
        def fill(r0, r1, is_ctx):
            r = 2 if is_ctx else 0
            h = _rms(x_ref[r0:r1, :], g_ref[...]) * (1.0 + mod_ref[r:r + 1, :]) + mod_ref[r + 1:r + 2, :]
            h_ref[r0:r1, :] = h.astype(BF16)
        _row_groups(i, tm, n_rows, n_lat, fill)

    o_ref[...] = jnp.dot(h_ref[...], w_ref[...], preferred_element_type=F32).astype(o_ref.dtype)


def _norm_matmul_call(xs, g, mod, w, n_lat):
    n_rows, d = xs.shape
    n_out = w.shape[1]
    tm = _pick(n_rows, (768, 512, 256))
    tn = _pick(n_out, (1408, 1024, 768, 512, 256))
    kern = functools.partial(_norm_matmul_kernel, tm=tm, n_rows=n_rows, n_lat=n_lat)
    return pl.pallas_call(
        kern,
        grid=(n_rows // tm, n_out // tn),
        in_specs=[pl.BlockSpec((tm, d), lambda i, j: (i, 0)),
                  pl.BlockSpec((1, d), lambda i, j: (0, 0)),
                  pl.BlockSpec((8, d), lambda i, j: (0, 0)),
                  pl.BlockSpec((d, tn), lambda i, j: (0, j))],
        out_specs=pl.BlockSpec((tm, tn), lambda i, j: (i, j)),
        out_shape=jax.ShapeDtypeStruct((n_rows, n_out), BF16),
        scratch_shapes=[pltpu.VMEM((tm, d), BF16)],
        compiler_params=_cparams(("arbitrary", "arbitrary")),
        name="norm_in_proj",
    )(xs, g.reshape(1, d), mod, w)


def _rope(xf, cos, sin):
    return xf * cos + pltpu.roll(xf, LANES // 2, 1) * sin


def _prep_kernel(cq_ref, ckv_ref, dq_ref, dk_ref, kr_ref, gq_ref, gkv_ref, wuq_ref, wk_ref, wv_ref,
                 cosd_ref, sind_ref, cosm_ref, sinm_ref,
                 qm_ref, km_ref, vm_ref, qd_ref, kd_ref, *, s_mla, s_diff):
    cosm = cosm_ref[...]
    sinm = sinm_ref[...]
    cosd = cosd_ref[...]
    sind = sind_ref[...]

    cqn = _rms(cq_ref[...].astype(F32), gq_ref[...]).astype(BF16)
    q = jnp.dot(cqn, wuq_ref[...], preferred_element_type=F32)
    for h in range(MLA_HEADS):
        base = h * MLA_QK_PAD
        qm_ref[h, :, 0:LANES] = (q[:, base:base + LANES] * s_mla).astype(BF16)
        qr = _rope(q[:, base + LANES:base + 2 * LANES], cosm, sinm)
        qm_ref[h, :, LANES:2 * LANES] = (qr * s_mla).astype(BF16)

    ckvn = _rms(ckv_ref[...].astype(F32), gkv_ref[...]).astype(BF16)
    kn = jnp.dot(ckvn, wk_ref[...], preferred_element_type=F32)
    vv = jnp.dot(ckvn, wv_ref[...], preferred_element_type=F32)
    kr = _rope(kr_ref[...].astype(F32), cosm, sinm).astype(BF16)
    for h in range(MLA_HEADS):
        km_ref[h, :, 0:LANES] = kn[:, h * LANES:(h + 1) * LANES].astype(BF16)
        km_ref[h, :, LANES:2 * LANES] = kr
        vm_ref[h] = vv[:, h * LANES:(h + 1) * LANES].astype(BF16)

    for hc in range(2 * DIFF_HEADS):
        xq = dq_ref[:, hc * LANES:(hc + 1) * LANES].astype(F32)
        qd_ref[hc] = (_rope(xq, cosd, sind) * s_diff).astype(BF16)
        xk = dk_ref[:, hc * LANES:(hc + 1) * LANES].astype(F32)
        kd_ref[hc] = _rope(xk, cosd, sind).astype(BF16)


def _prep_call(proj, gq, gkv, wuq, wk, wv, cosd, sind, cosm, sinm):
    n = proj.shape[0]
    tm = _pick(n, (384, 256, 128))
    nh, nd = MLA_HEADS, 2 * DIFF_HEADS
    kern = functools.partial(_prep_kernel, s_mla=float((MLA_NOPE + MLA_ROPE) ** -0.5 * LOG2E),
                             s_diff=float(DIFF_QK ** -0.5 * LOG2E))
    row = lambda w, c: pl.BlockSpec((tm, w), lambda i, c=c: (i, c))
    const = lambda a: pl.BlockSpec(a.shape, lambda i: (0,) * a.ndim)
    tab = pl.BlockSpec((tm, LANES), lambda i: (i, 0))
    return pl.pallas_call(
        kern,
        grid=(n // tm,),
        in_specs=[row(512, _OFF_CQ // 512), row(512, _OFF_CKV // 512), row(1024, _OFF_DQ // 1024),
                  row(1024, _OFF_DK // 1024), row(LANES, _OFF_KR // LANES),
                  const(gq), const(gkv), const(wuq), const(wk), const(wv), tab, tab, tab, tab],
        out_specs=[pl.BlockSpec((nh, tm, MLA_QK_PAD), lambda i: (0, i, 0)),
                   pl.BlockSpec((nh, tm, MLA_QK_PAD), lambda i: (0, i, 0)),
                   pl.BlockSpec((nh, tm, MLA_V), lambda i: (0, i, 0)),
                   pl.BlockSpec((nd, tm, DIFF_QK), lambda i: (0, i, 0)),
                   pl.BlockSpec((nd, tm, DIFF_QK), lambda i: (0, i, 0))],
        out_shape=[jax.ShapeDtypeStruct((nh, n, MLA_QK_PAD), BF16),
                   jax.ShapeDtypeStruct((nh, n, MLA_QK_PAD), BF16),
                   jax.ShapeDtypeStruct((nh, n, MLA_V), BF16),
                   jax.ShapeDtypeStruct((nd, n, DIFF_QK), BF16),
                   jax.ShapeDtypeStruct((nd, n, DIFF_QK), BF16)],
        compiler_params=_cparams(("arbitrary",)),
        name="qkv_prep",
    )(proj, proj, proj, proj, proj, gq, gkv, wuq, wk, wv, cosd, sind, cosm, sinm)


def _softmax_pv(q, k_at, v_at, tk, n_chunks, dv):
    tq = q.shape[0]
    m = jnp.full((tq, 1), -jnp.inf, F32)
    l = jnp.zeros((tq, 1), F32)
    acc = jnp.zeros((tq, dv), F32)
    for c in range(n_chunks):
        off = c * tk
        s = lax.dot_general(q, k_at(off), (((1,), (1,)), ((), ())), preferred_element_type=F32)
        m_new = jnp.maximum(m, jnp.max(s, axis=-1, keepdims=True))
        alpha = jnp.exp2(m - m_new)
        p = jnp.exp2(s - m_new)
        l = alpha * l + jnp.sum(p, axis=-1, keepdims=True)
        acc = alpha * acc + jnp.dot(p.astype(BF16), v_at(off), preferred_element_type=F32)
        m = m_new
    return acc / l


def _mla_attn_kernel(q_ref, k_ref, v_ref, *rest, tk, n_chunks):
    o_ref = rest[-1]
    o = _softmax_pv(q_ref[0],
                    lambda off: k_ref[0, pl.ds(off, tk), :],
                    lambda off: v_ref[0, pl.ds(off, tk), :],
                    tk, n_chunks, MLA_V)
    o_ref[...] = o.astype(o_ref.dtype)


def _mla_attn_call(qm, km, vm, q_row0, n_q, kv_row0, n_kv, n_tok, prev=None):
    tq = _pick(n_q, (512, 256, 128))
    tk = _pick(n_kv, (2816, 768, 512, 256, 128))
    assert q_row0 % tq == 0 and kv_row0 % n_kv == 0
    qb, kb = q_row0 // tq, kv_row0 // n_kv
    kern = functools.partial(_mla_attn_kernel, tk=tk, n_chunks=n_kv // tk)
    in_specs = [pl.BlockSpec((1, tq, MLA_QK_PAD), lambda h, i: (h, i + qb, 0)),
                pl.BlockSpec((1, n_kv, MLA_QK_PAD), lambda h, i: (h, kb, 0)),
                pl.BlockSpec((1, n_kv, MLA_V), lambda h, i: (h, kb, 0))]
    args = [qm, km, vm]
    aliases = {}
    if prev is not None:
        in_specs.append(pl.BlockSpec(memory_space=pl.ANY))
        args.append(prev)
        aliases = {3: 0}
    return pl.pallas_call(
        kern,
        grid=(MLA_HEADS, n_q // tq),
        in_specs=in_specs,
        out_specs=pl.BlockSpec((tq, MLA_V), lambda h, i: (i + qb, h)),
        out_shape=jax.ShapeDtypeStruct((n_tok, MLA_HEADS * MLA_V), BF16),
        input_output_aliases=aliases,
        compiler_params=_cparams(("arbitrary", "arbitrary")),
        name="mla_attention",
    )(*args)


def _diff_attn_kernel(q_ref, k_ref, v_ref, lam_ref, gs_ref, *rest, tk, n_chunks):
    o_ref = rest[-1]
    lam_init = lam_ref[4:5, 0:1]
    outs = []
    for comp in range(2):
        outs.append(_softmax_pv(q_ref[comp],
                                lambda off, comp=comp: k_ref[comp, pl.ds(off, tk), :],
                                lambda off: v_ref[pl.ds(off, tk), :],
                                tk, n_chunks, DIFF_V))
    lf = lam_ref[...]
    lam_full = (jnp.exp(jnp.sum(lf[0:1] * lf[1:2], axis=-1, keepdims=True))
                - jnp.exp(jnp.sum(lf[2:3] * lf[3:4], axis=-1, keepdims=True)) + lam_init)
    d = outs[0] - lam_full * outs[1]
    o_ref[...] = (_rms(d, gs_ref[...]) * (1.0 - lam_init)).astype(o_ref.dtype)


def _diff_attn_call(qd, kd, proj, lam, gsub, lam_init, q_row0, n_q, kv_row0, n_kv, n_tok, prev=None):
    tq = _pick(n_q, (512, 256, 128))
    tk = _pick(n_kv, (2816, 768, 512, 256, 128))
    assert q_row0 % tq == 0 and kv_row0 % n_kv == 0
    qb, kb = q_row0 // tq, kv_row0 // n_kv
    vcol = _OFF_DV // DIFF_V
    kern = functools.partial(_diff_attn_kernel, tk=tk, n_chunks=n_kv // tk)
    in_specs = [pl.BlockSpec((2, tq, DIFF_QK), lambda h, i: (h, i + qb, 0)),
                pl.BlockSpec((2, n_kv, DIFF_QK), lambda h, i: (h, kb, 0)),
                pl.BlockSpec((n_kv, DIFF_V), lambda h, i: (kb, vcol + h)),
                pl.BlockSpec((8, DIFF_QK), lambda h, i: (0, 0)),
                pl.BlockSpec((1, DIFF_V), lambda h, i: (0, 0))]
    lam8 = jnp.concatenate([lam, jnp.full((4, DIFF_QK), lam_init, F32)], axis=0)
    args = [qd, kd, proj, lam8, gsub.reshape(1, DIFF_V)]
    aliases = {}
    if prev is not None:
        in_specs.append(pl.BlockSpec(memory_space=pl.ANY))
        args.append(prev)
        aliases = {5: 0}
    return pl.pallas_call(
        kern,
        grid=(DIFF_HEADS, n_q // tq),
        in_specs=in_specs,
        out_specs=pl.BlockSpec((tq, DIFF_V), lambda h, i: (i + qb, h)),
        out_shape=jax.ShapeDtypeStruct((n_tok, DIFF_HEADS * DIFF_V), BF16),
        input_output_aliases=aliases,
        compiler_params=_cparams(("arbitrary", "arbitrary")),
        name="diff_attention",
    )(*args)


def _post_kernel(om_ref, od_ref, ga_ref, gb_ref, x_ref, wpm_ref, wpd_ref, wo_ref, g_ref, gate_ref, o_ref,
                 *, tm, n_rows, n_lat):
    i = pl.program_id(0)
    a = jnp.dot(om_ref[...], wpm_ref[...], preferred_element_type=F32)
    b = jnp.dot(od_ref[...], wpd_ref[...], preferred_element_type=F32)
    y = _sigmoid(ga_ref[...].astype(F32)) * a + _sigmoid(gb_ref[...].astype(F32)) * b
    z = jnp.dot(y.astype(BF16), wo_ref[...], preferred_element_type=F32)
    zn = _rms(z, g_ref[...])
    if n_rows > n_lat:
        gate = jnp.where(i * tm >= n_lat, gate_ref[1:2, :], gate_ref[0:1, :])
    else:
        gate = gate_ref[0:1, :]
    o_ref[...] = x_ref[...] + gate * zn


def _post_call(o_mla, o_d, proj, xs, wpm, wpd, wo, g, gate, n_rows, n_lat):
    d = xs.shape[1]
    tm = 256
    assert n_rows % tm == 0 and n_lat % tm == 0
    kern = functools.partial(_post_kernel, tm=tm, n_rows=n_rows, n_lat=n_lat)
    const = lambda a: pl.BlockSpec(a.shape, lambda i: (0,) * a.ndim, pipeline_mode=pl.Buffered(1))
    return pl.pallas_call(
        kern,
        grid=(n_rows // tm,),
        in_specs=[pl.BlockSpec((tm, o_mla.shape[1]), lambda i: (i, 0)),
                  pl.BlockSpec((tm, o_d.shape[1]), lambda i: (i, 0)),
                  pl.BlockSpec((tm, d), lambda i: (i, _OFF_GA // D_MODEL)),
                  pl.BlockSpec((tm, d), lambda i: (i, _OFF_GB // D_MODEL)),
                  pl.BlockSpec((tm, d), lambda i: (i, 0)),
                  const(wpm), const(wpd), const(wo),
                  pl.BlockSpec((1, d), lambda i: (0, 0)),
                  pl.BlockSpec((8, d), lambda i: (0, 0))],
        out_specs=pl.BlockSpec((tm, d), lambda i: (i, 0)),
        out_shape=jax.ShapeDtypeStruct((n_rows, d), F32),
        compiler_params=_cparams(("arbitrary",)),
        name="merge_out_proj",
    )(o_mla, o_d, proj, proj, xs, wpm, wpd, wo, g.reshape(1, d), gate)


def _router_kernel(x_ref, g_ref, mod_ref, wr_ref, o_ref, *, tm, n_rows, n_lat):
    i = pl.program_id(0)

    def fill(r0, r1, is_ctx):
        r = 3 if is_ctx else 0
        h = _rms(x_ref[r0:r1, :], g_ref[...]) * (1.0 + mod_ref[r:r + 1, :]) + mod_ref[r + 1:r + 2, :]
        logits = jnp.dot(h, wr_ref[...], preferred_element_type=F32, precision=lax.Precision.HIGHEST)
        lane = lax.broadcasted_iota(jnp.int32, logits.shape, 1).astype(F32)
        lg = jnp.where(lane < N_EXPERTS, logits, -jnp.inf)
        m1 = jnp.max(lg, axis=-1, keepdims=True)
        i1 = jnp.min(jnp.where(lg == m1, lane, float(LANES)), axis=-1, keepdims=True)
        lg2 = jnp.where(lane == i1, -jnp.inf, lg)
        m2 = jnp.max(lg2, axis=-1, keepdims=True)
        i2 = jnp.min(jnp.where(lg2 == m2, lane, float(LANES)), axis=-1, keepdims=True)
        e = jnp.exp(m2 - m1)
        den = 1.0 + e
        o_ref[r0:r1, :] = jnp.where(lane == i1, 1.0 / den, 0.0) + jnp.where(lane == i2, e / den, 0.0)

    _row_groups(i, tm, n_rows, n_lat, fill)


def _router_call(xs, g, mod, wr_pad, n_rows, n_lat):
    d = xs.shape[1]
    tm = _pick(n_rows, (768, 512, 256))
    kern = functools.partial(_router_kernel, tm=tm, n_rows=n_rows, n_lat=n_lat)
    return pl.pallas_call(
        kern,
        grid=(n_rows // tm,),
        in_specs=[pl.BlockSpec((tm, d), lambda i: (i, 0)),
                  pl.BlockSpec((1, d), lambda i: (0, 0)),
                  pl.BlockSpec((8, d), lambda i: (0, 0)),
                  pl.BlockSpec((d, LANES), lambda i: (0, 0))],
        out_specs=pl.BlockSpec((tm, LANES), lambda i: (i, 0)),
        out_shape=jax.ShapeDtypeStruct((n_rows, LANES), F32),
        compiler_params=_cparams(("arbitrary",)),
        name="moe_router",
    )(xs, g.reshape(1, d), mod, wr_pad)


def _ffn_kernel(*refs, tm, n_rows, n_lat, n_exp, n_k, gated):
    if gated:
        x_ref, g2_ref, mod_ref, w1_ref, w3_ref, w2_ref, g3_ref, gates_ref, o_ref, h_ref, acc_ref = refs
    else:
        x_ref, g2_ref, mod_ref, w1_ref, w3_ref, w2_ref, g3_ref, o_ref, h_ref, acc_ref = refs
    i = pl.program_id(0)
    e = pl.program_id(1)
    k = pl.program_id(2)

    @pl.when(jnp.logical_and(e == 0, k == 0))
    def _():
        def fill(r0, r1, is_ctx):
            r = 3 if is_ctx else 0
            h = _rms(x_ref[r0:r1, :], g2_ref[...]) * (1.0 + mod_ref[r:r + 1, :]) + mod_ref[r + 1:r + 2, :]
            h_ref[r0:r1, :] = h.astype(BF16)
        _row_groups(i, tm, n_rows, n_lat, fill)
        acc_ref[...] = jnp.zeros_like(acc_ref)

    h = h_ref[...]
    u1 = jnp.dot(h, w1_ref[0], preferred_element_type=F32)
    u3 = jnp.dot(h, w3_ref[0], preferred_element_type=F32)
    u = (u1 * _sigmoid(u1)) * u3
    if gated:
        gt = gates_ref[...]
        lane = lax.broadcasted_iota(jnp.int32, gt.shape, 1)
        u = u * jnp.sum(jnp.where(lane == e, gt, 0.0), axis=-1, keepdims=True)
    acc_ref[...] += jnp.dot(u.astype(BF16), w2_ref[0], preferred_element_type=F32)

    @pl.when(jnp.logical_and(e == n_exp - 1, k == n_k - 1))
    def _():
        def fin(r0, r1, is_ctx):
            r = 5 if is_ctx else 2
            fn = _rms(acc_ref[r0:r1, :], g3_ref[...])
            o_ref[r0:r1, :] = x_ref[r0:r1, :] + mod_ref[r:r + 1, :] * fn
        _row_groups(i, tm, n_rows, n_lat, fin)


def _ffn_call(xs, g2, mod, w1, w3, w2, g3, gates, n_rows, n_lat):
    d = xs.shape[1]
    n_exp, _, f = w1.shape
    tm = _pick(n_rows, (768, 512, 256))
    tf = _pick(f, (512, 256, 128))
    n_k = f // tf
    gated = gates is not None
    kern = functools.partial(_ffn_kernel, tm=tm, n_rows=n_rows, n_lat=n_lat, n_exp=n_exp, n_k=n_k, gated=gated)
    in_specs = [pl.BlockSpec((tm, d), lambda i, e, k: (i, 0)),
                pl.BlockSpec((1, d), lambda i, e, k: (0, 0)),
                pl.BlockSpec((8, d), lambda i, e, k: (0, 0)),
                pl.BlockSpec((1, d, tf), lambda i, e, k: (e, 0, k)),
                pl.BlockSpec((1, d, tf), lambda i, e, k: (e, 0, k)),
                pl.BlockSpec((1, tf, d), lambda i, e, k: (e, k, 0)),
                pl.BlockSpec((1, d), lambda i, e, k: (0, 0))]
    args = [xs, g2.reshape(1, d), mod, w1, w3, w2, g3.reshape(1, d)]
    if gated:
        in_specs.append(pl.BlockSpec((tm, LANES), lambda i, e, k: (i, 0)))
        args.append(gates)
    return pl.pallas_call(
        kern,
        grid=(n_rows // tm, n_exp, n_k),
        in_specs=in_specs,
        out_specs=pl.BlockSpec((tm, d), lambda i, e, k: (i, 0)),
        out_shape=jax.ShapeDtypeStruct((n_rows, d), F32),
        scratch_shapes=[pltpu.VMEM((tm, d), BF16), pltpu.VMEM((tm, d), F32)],
        compiler_params=_cparams(("arbitrary", "arbitrary", "arbitrary")),
        name="moe_ffn" if gated else "dense_ffn",
    )(*args)


def _diff_perm_cols(w, base):
    parts = []
    for b in range(8):
        o = base + b * LANES
        parts += [w[..., o:o + 32], w[..., o + 64:o + 96], w[..., o + 32:o + 64], w[..., o + 96:o + 128]]
    return parts


def _mla_rope_slot(w, base):
    z = jnp.zeros(w.shape[:-1] + (32,), w.dtype)
    return [w[..., base:base + 16], w[..., base + 32:base + 48], z,
            w[..., base + 16:base + 32], w[..., base + 48:base + 64], z]


def _prep_w_in(w_in):
    o_kr, o_dq, o_dk, o_dv, o_ga = 1024, 1088, 2112, 3136, 4160
    parts = [w_in[..., 0:1024]]
    parts += _diff_perm_cols(w_in, o_dq)
    parts += _diff_perm_cols(w_in, o_dk)
    parts += [w_in[..., o_dv:o_ga], w_in[..., o_ga:]]
    parts += _mla_rope_slot(w_in, o_kr)
    parts += [jnp.zeros(w_in.shape[:-1] + (LANES,), w_in.dtype)]
    return jnp.concatenate(parts, axis=-1).astype(BF16)


def _prep_w_uq(w_uq):
    parts = []
    per = MLA_NOPE + MLA_ROPE
    for h in range(MLA_HEADS):
        parts.append(w_uq[..., h * per:h * per + MLA_NOPE])
        parts += _mla_rope_slot(w_uq, h * per + MLA_NOPE)
    return jnp.concatenate(parts, axis=-1).astype(BF16)


def _prep_w_ukv(w_ukv):
    per = MLA_NOPE + MLA_V
    wk = jnp.concatenate([w_ukv[..., h * per:h * per + MLA_NOPE] for h in range(MLA_HEADS)], axis=-1)
    wv = jnp.concatenate([w_ukv[..., h * per + MLA_NOPE:(h + 1) * per] for h in range(MLA_HEADS)], axis=-1)
    return wk.astype(BF16), wv.astype(BF16)


def _rope_tables(n_lat, n_ctx):
    t = jnp.arange(n_lat, dtype=jnp.int32)
    row = (t // GRID_W).astype(F32)[:, None]
    col = (t % GRID_W).astype(F32)[:, None]

    def table(half, pad):
        freqs = ROPE_BASE ** (-jnp.arange(half, dtype=F32) / half)
        a = jnp.concatenate([row * freqs, col * freqs], axis=-1)
        z = jnp.zeros((n_lat, pad), F32)
        cos = jnp.concatenate([jnp.cos(a), 1.0 + z, jnp.cos(a), 1.0 + z], axis=-1)
        sin = jnp.concatenate([-jnp.sin(a), z, jnp.sin(a), z], axis=-1)
        cos = jnp.concatenate([cos, jnp.ones((n_ctx, LANES), F32)], axis=0)
        sin = jnp.concatenate([sin, jnp.zeros((n_ctx, LANES), F32)], axis=0)
        return cos, sin

    cosd, sind = table(DIFF_QK // 4, 0)
    cosm, sinm = table(MLA_ROPE // 4, 32)
    return cosd, sind, cosm, sinm


def kernel(x, c, ctx, c_ctx, w_mod, b_mod, g_norm, w_in, w_uq, g_qn, w_ukv, g_kvn, lam, g_sub, w_po_mla,
           w_po_diff, w_out, w1_dense, w3_dense, w2_dense, w_router, w1_moe, w3_moe, w2_moe):
    b, s, d = x.shape
    n_ctx = ctx.shape[1]
    depth = w_mod.shape[0]
    assert b == 1 and d == D_MODEL and s % GRID_W == 0
    n_tok = s + n_ctx

    xs = jnp.concatenate([x[0], ctx[0]], axis=0)
    c8 = jnp.concatenate([c, c_ctx[None, :], jnp.zeros((6, d), F32)], axis=0)
    mods = _mod_call(c8, w_mod, b_mod)

    w_in_p = _prep_w_in(w_in)
    w_uq_p = _prep_w_uq(w_uq)
    w_k_p, w_v_p = _prep_w_ukv(w_ukv)
    w_pm, w_pd, w_o = w_po_mla.astype(BF16), w_po_diff.astype(BF16), w_out.astype(BF16)
    w1d, w3d, w2d = w1_dense.astype(BF16), w3_dense.astype(BF16), w2_dense.astype(BF16)
    w1m, w3m, w2m = w1_moe.astype(BF16), w3_moe.astype(BF16), w2_moe.astype(BF16)
    wr_pad = jnp.pad(w_router, ((0, 0), (0, 0), (0, LANES - N_EXPERTS)))
    cosd, sind, cosm, sinm = _rope_tables(s, n_ctx)
    zrow = jnp.zeros((d,), F32)

    for layer in range(depth):
        need_ctx = layer < depth - 1
        lat, cx = mods[layer, 0], mods[layer, 1]
        seg = lambda v, k: v[k * d:(k + 1) * d]
        gn = g_norm[layer]

        mod1 = jnp.stack([seg(lat, 1), seg(lat, 0), seg(cx, 1), seg(cx, 0), zrow, zrow, zrow, zrow])
        proj = _norm_matmul_call(xs, gn[0], mod1, w_in_p[layer], s)
        qm, km, vm, qd, kd = _prep_call(proj, g_qn[layer].reshape(1, -1), g_kvn[layer].reshape(1, -1),
                                        w_uq_p[layer], w_k_p[layer], w_v_p[layer], cosd, sind, cosm, sinm)
        lam_init = 0.8 - 0.6 * math.exp(-0.3 * layer)
        o_mla = _mla_attn_call(qm, km, vm, 0, s, 0, n_tok, n_tok)
        o_d = _diff_attn_call(qd, kd, proj, lam[layer], g_sub[layer], lam_init, 0, s, 0, n_tok, n_tok)
        if need_ctx:
            o_mla = _mla_attn_call(qm, km, vm, s, n_ctx, s, n_ctx, n_tok, prev=o_mla)
            o_d = _diff_attn_call(qd, kd, proj, lam[layer], g_sub[layer], lam_init, s, n_ctx, s, n_ctx, n_tok,
                                  prev=o_d)
        n_rows = n_tok if need_ctx else s
        gate1 = jnp.stack([seg(lat, 2), seg(cx, 2), zrow, zrow, zrow, zrow, zrow, zrow])
        xs = _post_call(o_mla, o_d, proj, xs, w_pm[layer], w_pd[layer], w_o[layer], gn[1], gate1, n_rows, s)

        mod2 = jnp.stack([seg(lat, 4), seg(lat, 3), seg(lat, 5), seg(cx, 4), seg(cx, 3), seg(cx, 5), zrow, zrow])
        i = layer // 2
        if layer % 2 == 0:
            xs = _ffn_call(xs, gn[2], mod2, w1d[i:i + 1], w3d[i:i + 1], w2d[i:i + 1], gn[3], None, n_rows, s)
        else:
            gates = _router_call(xs, gn[2], mod2, wr_pad[i], n_rows, s)
            xs = _ffn_call(xs, gn[2], mod2, w1m[i], w3m[i], w2m[i], gn[3], gates, n_rows, s)
    return xs[:s][None]
```

```python
import functools
import math

import jax
import jax.numpy as jnp
from jax import lax
from jax.experimental import pallas as pl
from jax.experimental.pallas import tpu as pltpu

F32 = jnp.float32
BF16 = jnp.bfloat16

D_MODEL = 2048
GRID_W = 64
MLA_HEADS = 8
MLA_Q_RANK = 512
MLA_KV_RANK = 512
MLA_NOPE = 128
MLA_ROPE = 64
MLA_V = 128
MLA_QK_PAD = 256
DIFF_HEADS = 4
DIFF_QK = 128
DIFF_V = 256
N_EXPERTS = 8
ROPE_BASE = 10000.0
NORM_EPS = 1e-6
LANES = 128
LOG2E = math.log2(math.e)

_OFF_CQ, _OFF_CKV, _OFF_DQ, _OFF_DK, _OFF_DV, _OFF_GA, _OFF_GB, _OFF_KR = (
    0, 512, 1024, 2048, 3072, 4096, 6144, 8192)
D_IN_PAD = 8448

VMEM_LIMIT = 56 * 1024 * 1024


def _pick(n, cands):
    for c in cands:
        if n % c == 0:
            return c
    raise ValueError(f"no tile for {n} in {cands}")


def _cparams(sem):
    return pltpu.CompilerParams(dimension_semantics=sem, vmem_limit_bytes=VMEM_LIMIT)


def _rms(xf, g):
    ms = jnp.mean(xf * xf, axis=-1, keepdims=True)
    return xf * lax.rsqrt(ms + NORM_EPS) * g


def _sigmoid(v):
    return 1.0 / (1.0 + jnp.exp(-v))


def _row_groups(i, tm, n_rows, n_lat, fn):
    if n_rows <= n_lat:
        fn(0, tm, False)
        return
    nb = n_rows // tm
    b, off = divmod(n_lat, tm)
    if b > 0:
        pl.when(i < b)(lambda: fn(0, tm, False))
    if off:
        def _split():
            fn(0, off, False)
            fn(off, tm, True)
        pl.when(i == b)(_split)
        if nb > b + 1:
            pl.when(i > b)(lambda: fn(0, tm, True))
    else:
        pl.when(i >= b)(lambda: fn(0, tm, True))


def _mod_kernel(c_ref, w_ref, b_ref, o_ref):
    cv = c_ref[...]
    s = cv * _sigmoid(cv)
    o_ref[0] = jnp.dot(s, w_ref[0], preferred_element_type=F32,
                       precision=lax.Precision.HIGHEST) + b_ref[0]


def _mod_call(c8, w_mod, b_mod):
    depth, d, n6 = w_mod.shape
    tn = _pick(n6, (1536, 1024, 512, 256, 128))
    return pl.pallas_call(
        _mod_kernel,
        grid=(depth, n6 // tn),
        in_specs=[pl.BlockSpec((8, d), lambda l, j: (0, 0)),
                  pl.BlockSpec((1, d, tn), lambda l, j: (l, 0, j)),
                  pl.BlockSpec((1, 1, tn), lambda l, j: (l, 0, j))],
        out_specs=pl.BlockSpec((1, 8, tn), lambda l, j: (l, 0, j)),
        out_shape=jax.ShapeDtypeStruct((depth, 8, n6), F32),
        compiler_params=_cparams(("arbitrary", "arbitrary")),
        name="adaln_mod",
    )(c8, w_mod, b_mod.reshape(depth, 1, n6))


def _norm_matmul_kernel(x_ref, g_ref, mod_ref, w_ref, o_ref, h_ref, *, tm, n_rows, n_lat):
    i = pl.program_id(0)
    j = pl.program_id(1)

    @pl.when(j == 0)
    def _():
        def fill(r0, r1, is_ctx):
            r = 2 if is_ctx else 0
            h = _rms(x_ref[r0:r1, :], g_ref[...]) * (1.0 + mod_ref[r:r + 1, :]) + mod_ref[r + 1:r + 2, :]
            h_ref[r0:r1, :] = h.astype(BF16)
        _row_groups(i, tm, n_rows, n_lat, fill)

    o_ref[...] = jnp.dot(h_ref[...], w_ref[...], preferred_element_type=F32).astype(o_ref.dtype)


def _norm_matmul_call(xs, g, mod, w, n_lat):
    n_rows, d = xs.shape
    n_out = w.shape[1]
    tm = _pick(n_rows, (768, 512, 256))
    tn = _pick(n_out, (1408, 1024, 768, 512, 256))
    kern = functools.partial(_norm_matmul_kernel, tm=tm, n_rows=n_rows, n_lat=n_lat)
    return pl.pallas_call(
        kern,
        grid=(n_rows // tm, n_out // tn),
        in_specs=[pl.BlockSpec((tm, d), lambda i, j: (i, 0)),
                  pl.BlockSpec((1, d), lambda i, j: (0, 0)),
                  pl.BlockSpec((8, d), lambda i, j: (0, 0)),
                  pl.BlockSpec((d, tn), lambda i, j: (0, j))],
        out_specs=pl.BlockSpec((tm, tn), lambda i, j: (i, j)),
        out_shape=jax.ShapeDtypeStruct((n_rows, n_out), BF16),
        scratch_shapes=[pltpu.VMEM((tm, d), BF16)],
        compiler_params=_cparams(("arbitrary", "arbitrary")),
        name="norm_in_proj",
    )(xs, g.reshape(1, d), mod, w)


def _rope(xf, cos, sin):
    return xf * cos + pltpu.roll(xf, LANES // 2, 1) * sin


def _prep_kernel(cq_ref, ckv_ref, dq_ref, dk_ref, kr_ref, gq_ref, gkv_ref, wuq_ref, wk_ref, wv_ref,
                 cosd_ref, sind_ref, cosm_ref, sinm_ref,
                 qm_ref, km_ref, vm_ref, qd_ref, kd_ref, *, s_mla, s_diff):
    cosm = cosm_ref[...]
    sinm = sinm_ref[...]
    cosd = cosd_ref[...]
    sind = sind_ref[...]

    cqn = _rms(cq_ref[...].astype(F32), gq_ref[...]).astype(BF16)
    q = jnp.dot(cqn, wuq_ref[...], preferred_element_type=F32)
    for h in range(MLA_HEADS):
        base = h * MLA_QK_PAD
        qm_ref[h, :, 0:LANES] = (q[:, base:base + LANES] * s_mla).astype(BF16)
        qr = _rope(q[:, base + LANES:base + 2 * LANES], cosm, sinm)
        qm_ref[h, :, LANES:2 * LANES] = (qr * s_mla).astype(BF16)

    ckvn = _rms(ckv_ref[...].astype(F32), gkv_ref[...]).astype(BF16)
    kn = jnp.dot(ckvn, wk_ref[...], preferred_element_type=F32)
    vv = jnp.dot(ckvn, wv_ref[...], preferred_element_type=F32)
    kr = _rope(kr_ref[...].astype(F32), cosm, sinm).astype(BF16)
    for h in range(MLA_HEADS):
        km_ref[h, :, 0:LANES] = kn[:, h * LANES:(h + 1) * LANES].astype(BF16)
        km_ref[h, :, LANES:2 * LANES] = kr
        vm_ref[h] = vv[:, h * LANES:(h + 1) * LANES].astype(BF16)

    for hc in range(2 * DIFF_HEADS):
        xq = dq_ref[:, hc * LANES:(hc + 1) * LANES].astype(F32)
        qd_ref[hc] = (_rope(xq, cosd, sind) * s_diff).astype(BF16)
        xk = dk_ref[:, hc * LANES:(hc + 1) * LANES].astype(F32)
        kd_ref[hc] = _rope(xk, cosd, sind).astype(BF16)


def _prep_call(proj, gq, gkv, wuq, wk, wv, cosd, sind, cosm, sinm):
    n = proj.shape[0]
    tm = _pick(n, (384, 256, 128))
    nh, nd = MLA_HEADS, 2 * DIFF_HEADS
    kern = functools.partial(_prep_kernel, s_mla=float((MLA_NOPE + MLA_ROPE) ** -0.5 * LOG2E),
                             s_diff=float(DIFF_QK ** -0.5 * LOG2E))
    row = lambda w, c: pl.BlockSpec((tm, w), lambda i, c=c: (i, c))
    const = lambda a: pl.BlockSpec(a.shape, lambda i: (0,) * a.ndim)
    tab = pl.BlockSpec((tm, LANES), lambda i: (i, 0))
    return pl.pallas_call(
        kern,
        grid=(n // tm,),
        in_specs=[row(512, _OFF_CQ // 512), row(512, _OFF_CKV // 512), row(1024, _OFF_DQ // 1024),
                  row(1024, _OFF_DK // 1024), row(LANES, _OFF_KR // LANES),
                  const(gq), const(gkv), const(wuq), const(wk), const(wv), tab, tab, tab, tab],
        out_specs=[pl.BlockSpec((nh, tm, MLA_QK_PAD), lambda i: (0, i, 0)),
                   pl.BlockSpec((nh, tm, MLA_QK_PAD), lambda i: (0, i, 0)),
                   pl.BlockSpec((nh, tm, MLA_V), lambda i: (0, i, 0)),
                   pl.BlockSpec((nd, tm, DIFF_QK), lambda i: (0, i, 0)),
                   pl.BlockSpec((nd, tm, DIFF_QK), lambda i: (0, i, 0))],
        out_shape=[jax.ShapeDtypeStruct((nh, n, MLA_QK_PAD), BF16),
                   jax.ShapeDtypeStruct((nh, n, MLA_QK_PAD), BF16),
                   jax.ShapeDtypeStruct((nh, n, MLA_V), BF16),
                   jax.ShapeDtypeStruct((nd, n, DIFF_QK), BF16),
                   jax.ShapeDtypeStruct((nd, n, DIFF_QK), BF16)],
        compiler_params=_cparams(("arbitrary",)),
        name="qkv_prep",
    )(proj, proj, proj, proj, proj, gq, gkv, wuq, wk, wv, cosd, sind, cosm, sinm)


def _softmax_pv(q, k_at, v_at, tk, n_chunks, dv):
    tq = q.shape[0]
    m = jnp.full((tq, 1), -jnp.inf, F32)
    l = jnp.zeros((tq, 1), F32)
    acc = jnp.zeros((tq, dv), F32)
    for c in range(n_chunks):
        off = c * tk
        s = lax.dot_general(q, k_at(off), (((1,), (1,)), ((), ())), preferred_element_type=F32)
        m_new = jnp.maximum(m, jnp.max(s, axis=-1, keepdims=True))
        alpha = jnp.exp2(m - m_new)
        p = jnp.exp2(s - m_new)
        l = alpha * l + jnp.sum(p, axis=-1, keepdims=True)
        acc = alpha * acc + jnp.dot(p.astype(BF16), v_at(off), preferred_element_type=F32)
        m = m_new
    return acc / l


def _mla_attn_kernel(q_ref, k_ref, v_ref, o_ref, *, tk, n_chunks):
    o = _softmax_pv(q_ref[0],
                    lambda off: k_ref[0, pl.ds(off, tk), :],
                    lambda off: v_ref[0, pl.ds(off, tk), :],
                    tk, n_chunks, MLA_V)
    o_ref[...] = o.astype(o_ref.dtype)


def _mla_attn_call(qm, km, vm, q_row0, n_q, kv_row0, n_kv):
    tq = _pick(n_q, (512, 256, 128))
    tk = _pick(n_kv, (2816, 768, 512, 256, 128))
    assert q_row0 % tq == 0 and kv_row0 % n_kv == 0
    qb, kb = q_row0 // tq, kv_row0 // n_kv
    kern = functools.partial(_mla_attn_kernel, tk=tk, n_chunks=n_kv // tk)
    return pl.pallas_call(
        kern,
        grid=(MLA_HEADS, n_q // tq),
        in_specs=[pl.BlockSpec((1, tq, MLA_QK_PAD), lambda h, i: (h, i + qb, 0)),
                  pl.BlockSpec((1, n_kv, MLA_QK_PAD), lambda h, i: (h, kb, 0)),
                  pl.BlockSpec((1, n_kv, MLA_V), lambda h, i: (h, kb, 0))],
        out_specs=pl.BlockSpec((tq, MLA_V), lambda h, i: (i, h)),
        out_shape=jax.ShapeDtypeStruct((n_q, MLA_HEADS * MLA_V), BF16),
        compiler_params=_cparams(("arbitrary", "arbitrary")),
        name="mla_attention",
    )(qm, km, vm)


def _diff_attn_kernel(q_ref, k_ref, v_ref, lam_ref, gs_ref, o_ref, *, tk, n_chunks):
    lam_init = lam_ref[4:5, 0:1]
    outs = []
    for comp in range(2):
        outs.append(_softmax_pv(q_ref[comp],
                                lambda off, comp=comp: k_ref[comp, pl.ds(off, tk), :],
                                lambda off: v_ref[pl.ds(off, tk), :],
                                tk, n_chunks, DIFF_V))
    lf = lam_ref[...]
    lam_full = (jnp.exp(jnp.sum(lf[0:1] * lf[1:2], axis=-1, keepdims=True))
                - jnp.exp(jnp.sum(lf[2:3] * lf[3:4], axis=-1, keepdims=True)) + lam_init)
    d = outs[0] - lam_full * outs[1]
    o_ref[...] = (_rms(d, gs_ref[...]) * (1.0 - lam_init)).astype(o_ref.dtype)


def _diff_attn_call(qd, kd, proj, lam, gsub, lam_init, q_row0, n_q, kv_row0, n_kv):
    tq = _pick(n_q, (512, 256, 128))
    tk = _pick(n_kv, (2816, 768, 512, 256, 128))
    assert q_row0 % tq == 0 and kv_row0 % n_kv == 0
    qb, kb = q_row0 // tq, kv_row0 // n_kv
    vcol = _OFF_DV // DIFF_V
    kern = functools.partial(_diff_attn_kernel, tk=tk, n_chunks=n_kv // tk)
    in_specs = [pl.BlockSpec((2, tq, DIFF_QK), lambda h, i: (h, i + qb, 0)),
                pl.BlockSpec((2, n_kv, DIFF_QK), lambda h, i: (h, kb, 0)),
                pl.BlockSpec((n_kv, DIFF_V), lambda h, i: (kb, vcol + h)),
                pl.BlockSpec((8, DIFF_QK), lambda h, i: (0, 0)),
                pl.BlockSpec((1, DIFF_V), lambda h, i: (0, 0))]
    lam8 = jnp.concatenate([lam, jnp.full((4, DIFF_QK), lam_init, F32)], axis=0)
    return pl.pallas_call(
        kern,
        grid=(DIFF_HEADS, n_q // tq),
        in_specs=in_specs,
        out_specs=pl.BlockSpec((tq, DIFF_V), lambda h, i: (i, h)),
        out_shape=jax.ShapeDtypeStruct((n_q, DIFF_HEADS * DIFF_V), BF16),
        compiler_params=_cparams(("arbitrary", "arbitrary")),
        name="diff_attention",
    )(qd, kd, proj, lam8, gsub.reshape(1, DIFF_V))


def _post_kernel(*refs, tm, n_lat, with_ctx):
    if with_ctx:
        (om_ref, od_ref, omc_ref, odc_ref, ga_ref, gb_ref, x_ref, wpm_ref, wpd_ref, wo_ref, g_ref, gate_ref,
         o_ref) = refs
    else:
        om_ref, od_ref, ga_ref, gb_ref, x_ref, wpm_ref, wpd_ref, wo_ref, g_ref, gate_ref, o_ref = refs
    i = pl.program_id(0)
    om, od, gate = om_ref[...], od_ref[...], gate_ref[0:1, :]
    if with_ctx:
        is_ctx = i * tm >= n_lat
        om = jnp.where(is_ctx, omc_ref[...], om)
        od = jnp.where(is_ctx, odc_ref[...], od)
        gate = jnp.where(is_ctx, gate_ref[1:2, :], gate)
    a = jnp.dot(om, wpm_ref[...], preferred_element_type=F32)
    b = jnp.dot(od, wpd_ref[...], preferred_element_type=F32)
    y = _sigmoid(ga_ref[...].astype(F32)) * a + _sigmoid(gb_ref[...].astype(F32)) * b
    z = jnp.dot(y.astype(BF16), wo_ref[...], preferred_element_type=F32)
    o_ref[...] = x_ref[...] + gate * _rms(z, g_ref[...])


def _post_call(o_mla, o_d, o_mla_ctx, o_d_ctx, proj, xs, wpm, wpd, wo, g, gate, n_lat):
    d = xs.shape[1]
    tm = 256
    with_ctx = o_mla_ctx is not None
    n_rows = n_lat + (o_mla_ctx.shape[0] if with_ctx else 0)
    assert n_rows % tm == 0 and n_lat % tm == 0
    nb_lat = n_lat // tm
    kern = functools.partial(_post_kernel, tm=tm, n_lat=n_lat, with_ctx=with_ctx)
    const = lambda a: pl.BlockSpec(a.shape, lambda i: (0,) * a.ndim, pipeline_mode=pl.Buffered(1))
    lat = lambda a: pl.BlockSpec((tm, a.shape[1]), lambda i: (jnp.minimum(i, nb_lat - 1), 0))
    cxs = lambda a: pl.BlockSpec((tm, a.shape[1]), lambda i: (jnp.maximum(i - nb_lat, 0), 0))
    in_specs = [lat(o_mla), lat(o_d)]
    args = [o_mla, o_d]
    if with_ctx:
        in_specs += [cxs(o_mla_ctx), cxs(o_d_ctx)]
        args += [o_mla_ctx, o_d_ctx]
    in_specs += [pl.BlockSpec((tm, d), lambda i: (i, _OFF_GA // D_MODEL)),
                 pl.BlockSpec((tm, d), lambda i: (i, _OFF_GB // D_MODEL)),
                 pl.BlockSpec((tm, d), lambda i: (i, 0)),
                 const(wpm), const(wpd), const(wo),
                 pl.BlockSpec((1, d), lambda i: (0, 0)),
                 pl.BlockSpec((8, d), lambda i: (0, 0))]
    args += [proj, proj, xs, wpm, wpd, wo, g.reshape(1, d), gate]
    return pl.pallas_call(
        kern,
        grid=(n_rows // tm,),
        in_specs=in_specs,
        out_specs=pl.BlockSpec((tm, d), lambda i: (i, 0)),
        out_shape=jax.ShapeDtypeStruct((n_rows, d), F32),
        compiler_params=_cparams(("arbitrary",)),
        name="merge_out_proj",
    )(*args)


def _router_kernel(x_ref, g_ref, mod_ref, wr_ref, h_ref, route_ref, cnt_ref, carry_ref, *, tm, n_rows, n_lat):
    i = pl.program_id(0)

    @pl.when(i == 0)
    def _():
        carry_ref[...] = jnp.zeros_like(carry_ref)

    if n_rows > n_lat:
        is_ctx = i * tm >= n_lat
        scale = jnp.where(is_ctx, mod_ref[3:4, :], mod_ref[0:1, :])
        shift = jnp.where(is_ctx, mod_ref[4:5, :], mod_ref[1:2, :])
    else:
        scale, shift = mod_ref[0:1, :], mod_ref[1:2, :]
    h = _rms(x_ref[...], g_ref[...]) * (1.0 + scale) + shift
    h_ref[...] = h
    logits = jnp.dot(h, wr_ref[...], preferred_element_type=F32, precision=lax.Precision.HIGHEST)
    lane = lax.broadcasted_iota(jnp.int32, logits.shape, 1).astype(F32)
    lg = jnp.where(lane < N_EXPERTS, logits, -jnp.inf)
    m1 = jnp.max(lg, axis=-1, keepdims=True)
    i1 = jnp.min(jnp.where(lg == m1, lane, float(LANES)), axis=-1, keepdims=True)
    lg2 = jnp.where(lane == i1, -jnp.inf, lg)
    m2 = jnp.max(lg2, axis=-1, keepdims=True)
    i2 = jnp.min(jnp.where(lg2 == m2, lane, float(LANES)), axis=-1, keepdims=True)
    e = jnp.exp(m2 - m1)
    den = 1.0 + e

    member = jnp.where(jnp.logical_or(lane == i1, lane == i2), 1.0, 0.0)
    rr = lax.broadcasted_iota(jnp.int32, (tm, tm), 0)
    cc = lax.broadcasted_iota(jnp.int32, (tm, tm), 1)
    lower = jnp.where(rr > cc, 1.0, 0.0).astype(BF16)
    rank = jnp.dot(lower, member.astype(BF16), preferred_element_type=F32) + carry_ref[0:1, :]
    r1 = jnp.sum(jnp.where(lane == i1, rank, 0.0), axis=-1, keepdims=True)
    r2 = jnp.sum(jnp.where(lane == i2, rank, 0.0), axis=-1, keepdims=True)
    total = carry_ref[0:1, :] + jnp.sum(member, axis=0, keepdims=True)
    carry_ref[...] = jnp.broadcast_to(total, carry_ref.shape)
    cnt_ref[...] = jnp.broadcast_to(total, cnt_ref.shape)

    cols = (i1, i2, 1.0 / den, e / den, r1, r2)
    route = jnp.zeros_like(logits)
    for j, v in enumerate(cols):
        route = jnp.where(lane == float(j), v, route)
    route_ref[...] = route


def _router_call(xs, g, mod, wr_pad, n_rows, n_lat):
    d = xs.shape[1]
    tm = 256
    assert n_rows % tm == 0 and n_lat % tm == 0
    kern = functools.partial(_router_kernel, tm=tm, n_rows=n_rows, n_lat=n_lat)
    return pl.pallas_call(
        kern,
        grid=(n_rows // tm,),
        in_specs=[pl.BlockSpec((tm, d), lambda i: (i, 0)),
                  pl.BlockSpec((1, d), lambda i: (0, 0)),
                  pl.BlockSpec((8, d), lambda i: (0, 0)),
                  pl.BlockSpec((d, LANES), lambda i: (0, 0))],
        out_specs=[pl.BlockSpec((tm, d), lambda i: (i, 0)),
                   pl.BlockSpec((tm, LANES), lambda i: (i, 0)),
                   pl.BlockSpec((8, LANES), lambda i: (0, 0))],
        out_shape=[jax.ShapeDtypeStruct((n_rows, d), F32),
                   jax.ShapeDtypeStruct((n_rows, LANES), F32),
                   jax.ShapeDtypeStruct((8, LANES), F32)],
        scratch_shapes=[pltpu.VMEM((8, LANES), F32)],
        compiler_params=_cparams(("arbitrary",)),
        name="moe_router",
    )(xs, g.reshape(1, d), mod, wr_pad)


MOE_TB = 512
MOE_TF = 256
DMA_UNROLL = 8


def _dispatch_kernel(d1_ref, d2_ref, h_ref, xs_in_ref, xs_ref, sem, *, tm):
    del xs_in_ref
    base = pl.program_id(0) * tm

    def issue(t, carry):
        src = h_ref.at[pl.ds(base + t, 1), :]
        pltpu.make_async_copy(src, xs_ref.at[pl.ds(d1_ref[0, 0, t], 1), :], sem).start()
        pltpu.make_async_copy(src, xs_ref.at[pl.ds(d2_ref[0, 0, t], 1), :], sem).start()
        return carry

    lax.fori_loop(0, tm, issue, 0, unroll=DMA_UNROLL)

    def drain(t, carry):
        cp = pltpu.make_async_copy(h_ref.at[pl.ds(0, 1), :], xs_ref.at[pl.ds(0, 1), :], sem)
        cp.wait()
        cp.wait()
        return carry

    lax.fori_loop(0, tm, drain, 0, unroll=DMA_UNROLL)


def _dispatch_call(h, dest1, dest2, n_slots):
    n, d = h.shape
    tm = 256
    kern = functools.partial(_dispatch_kernel, tm=tm)
    idx = pl.BlockSpec((1, 1, tm), lambda i: (i, 0, 0), memory_space=pltpu.SMEM)
    anyspec = pl.BlockSpec(memory_space=pl.ANY)
    return pl.pallas_call(
        kern,
        grid=(n // tm,),
        in_specs=[idx, idx, anyspec, anyspec],
        out_specs=anyspec,
        out_shape=jax.ShapeDtypeStruct((n_slots, d), F32),
        scratch_shapes=[pltpu.SemaphoreType.DMA(())],
        input_output_aliases={3: 0},
        compiler_params=_cparams(("arbitrary",)),
        name="moe_dispatch",
    )(dest1.reshape(n // tm, 1, tm), dest2.reshape(n // tm, 1, tm), h, jnp.zeros((n_slots, d), F32))


def _grouped_ffn_kernel(be_ref, nv_ref, x_ref, w1_ref, w3_ref, w2_ref, o_ref, xb_ref):
    del be_ref
    b = pl.program_id(0)
    k = pl.program_id(1)
    valid = b < nv_ref[0]

    @pl.when(jnp.logical_and(valid, k == 0))
    def _():
        xb_ref[...] = x_ref[...].astype(BF16)

    @pl.when(valid)
    def _():
        xb = xb_ref[...]
        u1 = jnp.dot(xb, w1_ref[0].astype(BF16), preferred_element_type=F32)
        u3 = jnp.dot(xb, w3_ref[0].astype(BF16), preferred_element_type=F32)
        u = (u1 * _sigmoid(u1)) * u3
        part = jnp.dot(u.astype(BF16), w2_ref[0].astype(BF16), preferred_element_type=F32)

        @pl.when(k == 0)
        def _():
            o_ref[...] = part

        @pl.when(k > 0)
        def _():
            o_ref[...] += part

    @pl.when(jnp.logical_and(jnp.logical_not(valid), k == 0))
    def _():
        o_ref[...] = jnp.zeros_like(o_ref)


def _grouped_ffn_call(xs, w1, w3, w2, block_expert, n_valid):
    n_slots, d = xs.shape
    f = w1.shape[2]
    tb, tf = MOE_TB, MOE_TF
    n_k = f // tf
    assert n_slots % tb == 0 and f % tf == 0

    def blk(b, nv):
        return jnp.minimum(b, nv[0] - 1)

    def kk(b, k, nv):
        return jnp.where(b < nv[0], k, n_k - 1)

    grid_spec = pltpu.PrefetchScalarGridSpec(
        num_scalar_prefetch=2,
        grid=(n_slots // tb, n_k),
        in_specs=[pl.BlockSpec((tb, d), lambda b, k, be, nv: (blk(b, nv), 0)),
                  pl.BlockSpec((1, d, tf), lambda b, k, be, nv: (be[blk(b, nv)], 0, kk(b, k, nv))),
                  pl.BlockSpec((1, d, tf), lambda b, k, be, nv: (be[blk(b, nv)], 0, kk(b, k, nv))),
                  pl.BlockSpec((1, tf, d), lambda b, k, be, nv: (be[blk(b, nv)], kk(b, k, nv), 0))],
        out_specs=pl.BlockSpec((tb, d), lambda b, k, be, nv: (b, 0)),
        scratch_shapes=[pltpu.VMEM((tb, d), BF16)])
    return pl.pallas_call(
        _grouped_ffn_kernel,
        grid_spec=grid_spec,
        out_shape=jax.ShapeDtypeStruct((n_slots, d), F32),
        compiler_params=_cparams(("arbitrary", "arbitrary")),
        name="moe_grouped_ffn",
    )(block_expert, n_valid, xs, w1, w3, w2)


def _combine_kernel(d1_ref, d2_ref, route_ref, x_ref, g3_ref, mod_ref, y_ref, o_ref, ya_ref, yb_ref, sem,
                    *, tm, n_rows, n_lat):
    i = pl.program_id(0)

    def issue(t, carry):
        pltpu.make_async_copy(y_ref.at[pl.ds(d1_ref[0, 0, t], 1), :], ya_ref.at[pl.ds(t, 1), :], sem).start()
        pltpu.make_async_copy(y_ref.at[pl.ds(d2_ref[0, 0, t], 1), :], yb_ref.at[pl.ds(t, 1), :], sem).start()
        return carry

    lax.fori_loop(0, tm, issue, 0, unroll=DMA_UNROLL)

    def drain(t, carry):
        cp = pltpu.make_async_copy(y_ref.at[pl.ds(0, 1), :], ya_ref.at[pl.ds(0, 1), :], sem)
        cp.wait()
        cp.wait()
        return carry

    lax.fori_loop(0, tm, drain, 0, unroll=DMA_UNROLL)

    r = route_ref[...]
    f = r[:, 2:3] * ya_ref[...] + r[:, 3:4] * yb_ref[...]
    if n_rows > n_lat:
        gate = jnp.where(i * tm >= n_lat, mod_ref[5:6, :], mod_ref[2:3, :])
    else:
        gate = mod_ref[2:3, :]
    o_ref[...] = x_ref[...] + gate * _rms(f, g3_ref[...])


def _combine_call(y, dest1, dest2, route, xs, g3, mod, n_rows, n_lat):
    d = xs.shape[1]
    tm = 256
    kern = functools.partial(_combine_kernel, tm=tm, n_rows=n_rows, n_lat=n_lat)
    idx = pl.BlockSpec((1, 1, tm), lambda i: (i, 0, 0), memory_space=pltpu.SMEM)
    return pl.pallas_call(
        kern,
        grid=(n_rows // tm,),
        in_specs=[idx, idx,
                  pl.BlockSpec((tm, LANES), lambda i: (i, 0)),
                  pl.BlockSpec((tm, d), lambda i: (i, 0)),
                  pl.BlockSpec((1, d), lambda i: (0, 0)),
                  pl.BlockSpec((8, d), lambda i: (0, 0)),
                  pl.BlockSpec(memory_space=pl.ANY)],
        out_specs=pl.BlockSpec((tm, d), lambda i: (i, 0)),
        out_shape=jax.ShapeDtypeStruct((n_rows, d), F32),
        scratch_shapes=[pltpu.VMEM((tm, d), F32), pltpu.VMEM((tm, d), F32), pltpu.SemaphoreType.DMA(())],
        compiler_params=_cparams(("arbitrary",)),
        name="moe_combine",
    )(dest1.reshape(n_rows // tm, 1, tm), dest2.reshape(n_rows // tm, 1, tm), route, xs, g3.reshape(1, d), mod, y)


def _moe_call(xs, g2, mod, wr_pad, w1, w3, w2, g3, n_rows, n_lat):
    h, route, cnt = _router_call(xs, g2, mod, wr_pad, n_rows, n_lat)
    tb = MOE_TB
    n_slots = -(-(2 * n_rows + N_EXPERTS * (tb - 1)) // tb) * tb
    counts = cnt[0, :N_EXPERTS].astype(jnp.int32)
    padded = (counts + tb - 1) // tb * tb
    ends = jnp.cumsum(padded)
    offs = ends - padded
    e1, e2 = route[:, 0].astype(jnp.int32), route[:, 1].astype(jnp.int32)
    dest1 = offs[e1] + route[:, 4].astype(jnp.int32)
    dest2 = offs[e2] + route[:, 5].astype(jnp.int32)
    blk_start = jnp.arange(n_slots // tb, dtype=jnp.int32) * tb
    block_expert = jnp.minimum(jnp.sum(blk_start[:, None] >= ends[None, :], axis=1), N_EXPERTS - 1)
    n_valid = (ends[-1:] // tb).astype(jnp.int32)
    xg = _dispatch_call(h, dest1, dest2, n_slots)
    y = _grouped_ffn_call(xg, w1, w3, w2, block_expert.astype(jnp.int32), n_valid)
    return _combine_call(y, dest1, dest2, route, xs, g3, mod, n_rows, n_lat)


def _ffn_kernel(x_ref, g2_ref, mod_ref, w1_ref, w3_ref, w2_ref, g3_ref, o_ref, h_ref, acc_ref,
                *, tm, n_rows, n_lat, n_k):
    i = pl.program_id(0)
    k = pl.program_id(1)

    @pl.when(k == 0)
    def _():
        def fill(r0, r1, is_ctx):
            r = 3 if is_ctx else 0
            h = _rms(x_ref[r0:r1, :], g2_ref[...]) * (1.0 + mod_ref[r:r + 1, :]) + mod_ref[r + 1:r + 2, :]
            h_ref[r0:r1, :] = h.astype(BF16)
        _row_groups(i, tm, n_rows, n_lat, fill)
        acc_ref[...] = jnp.zeros_like(acc_ref)

    h = h_ref[...]
    u1 = jnp.dot(h, w1_ref[...], preferred_element_type=F32)
    u3 = jnp.dot(h, w3_ref[...], preferred_element_type=F32)
    u = (u1 * _sigmoid(u1)) * u3
    acc_ref[...] += jnp.dot(u.astype(BF16), w2_ref[...], preferred_element_type=F32)

    @pl.when(k == n_k - 1)
    def _():
        def fin(r0, r1, is_ctx):
            r = 5 if is_ctx else 2
            fn = _rms(acc_ref[r0:r1, :], g3_ref[...])
            o_ref[r0:r1, :] = x_ref[r0:r1, :] + mod_ref[r:r + 1, :] * fn
        _row_groups(i, tm, n_rows, n_lat, fin)


def _ffn_call(xs, g2, mod, w1, w3, w2, g3, n_rows, n_lat):
    d = xs.shape[1]
    f = w1.shape[1]
    tm = _pick(n_rows, (768, 512, 256))
    tf = _pick(f, (512, 256, 128))
    n_k = f // tf
    kern = functools.partial(_ffn_kernel, tm=tm, n_rows=n_rows, n_lat=n_lat, n_k=n_k)
    return pl.pallas_call(
        kern,
        grid=(n_rows // tm, n_k),
        in_specs=[pl.BlockSpec((tm, d), lambda i, k: (i, 0)),
                  pl.BlockSpec((1, d), lambda i, k: (0, 0)),
                  pl.BlockSpec((8, d), lambda i, k: (0, 0)),
                  pl.BlockSpec((d, tf), lambda i, k: (0, k)),
                  pl.BlockSpec((d, tf), lambda i, k: (0, k)),
                  pl.BlockSpec((tf, d), lambda i, k: (k, 0)),
                  pl.BlockSpec((1, d), lambda i, k: (0, 0))],
        out_specs=pl.BlockSpec((tm, d), lambda i, k: (i, 0)),
        out_shape=jax.ShapeDtypeStruct((n_rows, d), F32),
        scratch_shapes=[pltpu.VMEM((tm, d), BF16), pltpu.VMEM((tm, d), F32)],
        compiler_params=_cparams(("arbitrary", "arbitrary")),
        name="dense_ffn",
    )(xs, g2.reshape(1, d), mod, w1, w3, w2, g3.reshape(1, d))


def _diff_perm_cols(w, base):
    parts = []
    for b in range(8):
        o = base + b * LANES
        parts += [w[..., o:o + 32], w[..., o + 64:o + 96], w[..., o + 32:o + 64], w[..., o + 96:o + 128]]
    return parts


def _mla_rope_slot(w, base):
    z = jnp.zeros(w.shape[:-1] + (32,), w.dtype)
    return [w[..., base:base + 16], w[..., base + 32:base + 48], z,
            w[..., base + 16:base + 32], w[..., base + 48:base + 64], z]


def _prep_w_in(w_in):
    o_kr, o_dq, o_dk, o_dv, o_ga = 1024, 1088, 2112, 3136, 4160
    parts = [w_in[..., 0:1024]]
    parts += _diff_perm_cols(w_in, o_dq)
    parts += _diff_perm_cols(w_in, o_dk)
    parts += [w_in[..., o_dv:o_ga], w_in[..., o_ga:]]
    parts += _mla_rope_slot(w_in, o_kr)
    parts += [jnp.zeros(w_in.shape[:-1] + (LANES,), w_in.dtype)]
    return jnp.concatenate(parts, axis=-1).astype(BF16)


def _prep_w_uq(w_uq):
    parts = []
    per = MLA_NOPE + MLA_ROPE
    for h in range(MLA_HEADS):
        parts.append(w_uq[..., h * per:h * per + MLA_NOPE])
        parts += _mla_rope_slot(w_uq, h * per + MLA_NOPE)
    return jnp.concatenate(parts, axis=-1).astype(BF16)


def _prep_w_ukv(w_ukv):
    per = MLA_NOPE + MLA_V
    wk = jnp.concatenate([w_ukv[..., h * per:h * per + MLA_NOPE] for h in range(MLA_HEADS)], axis=-1)
    wv = jnp.concatenate([w_ukv[..., h * per + MLA_NOPE:(h + 1) * per] for h in range(MLA_HEADS)], axis=-1)
    return wk.astype(BF16), wv.astype(BF16)


def _rope_tables(n_lat, n_ctx):
    t = jnp.arange(n_lat, dtype=jnp.int32)
    row = (t // GRID_W).astype(F32)[:, None]
    col = (t % GRID_W).astype(F32)[:, None]

    def table(half, pad):
        freqs = ROPE_BASE ** (-jnp.arange(half, dtype=F32) / half)
        a = jnp.concatenate([row * freqs, col * freqs], axis=-1)
        z = jnp.zeros((n_lat, pad), F32)
        cos = jnp.concatenate([jnp.cos(a), 1.0 + z, jnp.cos(a), 1.0 + z], axis=-1)
        sin = jnp.concatenate([-jnp.sin(a), z, jnp.sin(a), z], axis=-1)
        cos = jnp.concatenate([cos, jnp.ones((n_ctx, LANES), F32)], axis=0)
        sin = jnp.concatenate([sin, jnp.zeros((n_ctx, LANES), F32)], axis=0)
        return cos, sin

    cosd, sind = table(DIFF_QK // 4, 0)
    cosm, sinm = table(MLA_ROPE // 4, 32)
    return cosd, sind, cosm, sinm


def kernel(x, c, ctx, c_ctx, w_mod, b_mod, g_norm, w_in, w_uq, g_qn, w_ukv, g_kvn, lam, g_sub, w_po_mla,
           w_po_diff, w_out, w1_dense, w3_dense, w2_dense, w_router, w1_moe, w3_moe, w2_moe):
    b, s, d = x.shape
    n_ctx = ctx.shape[1]
    depth = w_mod.shape[0]
    assert b == 1 and d == D_MODEL and s % GRID_W == 0
    n_tok = s + n_ctx

    xs = jnp.concatenate([x[0], ctx[0]], axis=0)
    c8 = jnp.concatenate([c, c_ctx[None, :], jnp.zeros((6, d), F32)], axis=0)
    mods = _mod_call(c8, w_mod, b_mod)

    w_in_p = _prep_w_in(w_in)
    w_uq_p = _prep_w_uq(w_uq)
    w_k_p, w_v_p = _prep_w_ukv(w_ukv)
    w_pm, w_pd, w_o = w_po_mla.astype(BF16), w_po_diff.astype(BF16), w_out.astype(BF16)
    w1d, w3d, w2d = w1_dense.astype(BF16), w3_dense.astype(BF16), w2_dense.astype(BF16)
    wr_pad = jnp.pad(w_router, ((0, 0), (0, 0), (0, LANES - N_EXPERTS)))
    cosd, sind, cosm, sinm = _rope_tables(s, n_ctx)
    zrow = jnp.zeros((d,), F32)

    for layer in range(depth):
        need_ctx = layer < depth - 1
        lat, cx = mods[layer, 0], mods[layer, 1]
        seg = lambda v, k: v[k * d:(k + 1) * d]
        gn = g_norm[layer]

        mod1 = jnp.stack([seg(lat, 1), seg(lat, 0), seg(cx, 1), seg(cx, 0), zrow, zrow, zrow, zrow])
        proj = _norm_matmul_call(xs, gn[0], mod1, w_in_p[layer], s)
        qm, km, vm, qd, kd = _prep_call(proj, g_qn[layer].reshape(1, -1), g_kvn[layer].reshape(1, -1),
                                        w_uq_p[layer], w_k_p[layer], w_v_p[layer], cosd, sind, cosm, sinm)
        lam_init = 0.8 - 0.6 * math.exp(-0.3 * layer)
        o_mla = _mla_attn_call(qm, km, vm, 0, s, 0, n_tok)
        o_d = _diff_attn_call(qd, kd, proj, lam[layer], g_sub[layer], lam_init, 0, s, 0, n_tok)
        o_mla_c = o_d_c = None
        if need_ctx:
            o_mla_c = _mla_attn_call(qm, km, vm, s, n_ctx, s, n_ctx)
            o_d_c = _diff_attn_call(qd, kd, proj, lam[layer], g_sub[layer], lam_init, s, n_ctx, s, n_ctx)
        n_rows = n_tok if need_ctx else s
        gate1 = jnp.stack([seg(lat, 2), seg(cx, 2), zrow, zrow, zrow, zrow, zrow, zrow])
        xs = _post_call(o_mla, o_d, o_mla_c, o_d_c, proj, xs, w_pm[layer], w_pd[layer], w_o[layer], gn[1],
                        gate1, s)

        mod2 = jnp.stack([seg(lat, 4), seg(lat, 3), seg(lat, 5), seg(cx, 4), seg(cx, 3), seg(cx, 5), zrow, zrow])
        i = layer // 2
        if layer % 2 == 0:
            xs = _ffn_call(xs, gn[2], mod2, w1d[i], w3d[i], w2d[i], gn[3], n_rows, s)
        else:
            xs = _moe_call(xs, gn[2], mod2, wr_pad[i], w1_moe[i], w3_moe[i], w2_moe[i], gn[3], n_rows, s)
    return xs[:s][None]
```

```python
import functools
import math

import jax
import jax.numpy as jnp
from jax import lax
from jax.experimental import pallas as pl
from jax.experimental.pallas import tpu as pltpu

F32 = jnp.float32
BF16 = jnp.bfloat16

D_MODEL = 2048
GRID_W = 64
MLA_HEADS = 8
MLA_Q_RANK = 512
MLA_KV_RANK = 512
MLA_NOPE = 128
MLA_ROPE = 64
MLA_V = 128
MLA_QK_PAD = 256
DIFF_HEADS = 4
DIFF_QK = 128
DIFF_V = 256
N_EXPERTS = 8
ROPE_BASE = 10000.0
NORM_EPS = 1e-6
LANES = 128
LOG2E = math.log2(math.e)

_OFF_CQ, _OFF_CKV, _OFF_DQ, _OFF_DK, _OFF_DV, _OFF_GA, _OFF_GB, _OFF_KR = (
    0, 512, 1024, 2048, 3072, 4096, 6144, 8192)
D_IN_PAD = 8448

VMEM_LIMIT = 56 * 1024 * 1024


def _pick(n, cands):
    for c in cands:
        if n % c == 0:
            return c
    raise ValueError(f"no tile for {n} in {cands}")


def _cparams(sem):
    return pltpu.CompilerParams(dimension_semantics=sem, vmem_limit_bytes=VMEM_LIMIT)


def _rms(xf, g):
    ms = jnp.mean(xf * xf, axis=-1, keepdims=True)
    return xf * lax.rsqrt(ms + NORM_EPS) * g


def _sigmoid(v):
    return 1.0 / (1.0 + jnp.exp(-v))


def _row_groups(i, tm, n_rows, n_lat, fn):
    if n_rows <= n_lat:
        fn(0, tm, False)
        return
    nb = n_rows // tm
    b, off = divmod(n_lat, tm)
    if b > 0:
        pl.when(i < b)(lambda: fn(0, tm, False))
    if off:
        def _split():
            fn(0, off, False)
            fn(off, tm, True)
        pl.when(i == b)(_split)
        if nb > b + 1:
            pl.when(i > b)(lambda: fn(0, tm, True))
    else:
        pl.when(i >= b)(lambda: fn(0, tm, True))


def _mod_kernel(c_ref, w_ref, b_ref, o_ref):
    cv = c_ref[...]
    s = cv * _sigmoid(cv)
    o_ref[0] = jnp.dot(s, w_ref[0], preferred_element_type=F32,
                       precision=lax.Precision.HIGHEST) + b_ref[0]


def _mod_call(c8, w_mod, b_mod):
    depth, d, n6 = w_mod.shape
    tn = _pick(n6, (1536, 1024, 512, 256, 128))
    return pl.pallas_call(
        _mod_kernel,
        grid=(depth, n6 // tn),
        in_specs=[pl.BlockSpec((8, d), lambda l, j: (0, 0)),
                  pl.BlockSpec((1, d, tn), lambda l, j: (l, 0, j)),
                  pl.BlockSpec((1, 1, tn), lambda l, j: (l, 0, j))],
        out_specs=pl.BlockSpec((1, 8, tn), lambda l, j: (l, 0, j)),
        out_shape=jax.ShapeDtypeStruct((depth, 8, n6), F32),
        compiler_params=_cparams(("arbitrary", "arbitrary")),
        name="adaln_mod",
    )(c8, w_mod, b_mod.reshape(depth, 1, n6))


def _norm_matmul_kernel(x_ref, g_ref, mod_ref, w_ref, o_ref, h_ref, *, tm, n_rows, n_lat):
    i = pl.program_id(0)
    j = pl.program_id(1)

    @pl.when(j == 0)
    def _():
        def fill(r0, r1, is_ctx):
            r = 2 if is_ctx else 0
            h = _rms(x_ref[r0:r1, :], g_ref[...]) * (1.0 + mod_ref[r:r + 1, :]) + mod_ref[r + 1:r + 2, :]
            h_ref[r0:r1, :] = h.astype(BF16)
        _row_groups(i, tm, n_rows, n_lat, fill)

    o_ref[...] = jnp.dot(h_ref[...], w_ref[...], preferred_element_type=F32).astype(o_ref.dtype)


def _norm_matmul_call(xs, g, mod, w, n_lat):
    n_rows, d = xs.shape
    n_out = w.shape[1]
    tm = _pick(n_rows, (768, 512, 256))
    tn = _pick(n_out, (1408, 1024, 768, 512, 256))
    kern = functools.partial(_norm_matmul_kernel, tm=tm, n_rows=n_rows, n_lat=n_lat)
    return pl.pallas_call(
        kern,
        grid=(n_rows // tm, n_out // tn),
        in_specs=[pl.BlockSpec((tm, d), lambda i, j: (i, 0)),
                  pl.BlockSpec((1, d), lambda i, j: (0, 0)),
                  pl.BlockSpec((8, d), lambda i, j: (0, 0)),
                  pl.BlockSpec((d, tn), lambda i, j: (0, j))],
        out_specs=pl.BlockSpec((tm, tn), lambda i, j: (i, j)),
        out_shape=jax.ShapeDtypeStruct((n_rows, n_out), BF16),
        scratch_shapes=[pltpu.VMEM((tm, d), BF16)],
        compiler_params=_cparams(("arbitrary", "arbitrary")),
        name="norm_in_proj",
    )(xs, g.reshape(1, d), mod, w)


def _rope(xf, cos, sin):
    return xf * cos + pltpu.roll(xf, LANES // 2, 1) * sin


def _prep_kernel(cq_ref, ckv_ref, dq_ref, dk_ref, kr_ref, gq_ref, gkv_ref, wuq_ref, wk_ref, wv_ref,
                 cosd_ref, sind_ref, cosm_ref, sinm_ref,
                 qm_ref, km_ref, vm_ref, qd_ref, kd_ref, *, s_mla, s_diff):
    cosm = cosm_ref[...]
    sinm = sinm_ref[...]
    cosd = cosd_ref[...]
    sind = sind_ref[...]

    cqn = _rms(cq_ref[...].astype(F32), gq_ref[...]).astype(BF16)
    q = jnp.dot(cqn, wuq_ref[...], preferred_element_type=F32)
    for h in range(MLA_HEADS):
        base = h * MLA_QK_PAD
        qm_ref[h, :, 0:LANES] = (q[:, base:base + LANES] * s_mla).astype(BF16)
        qr = _rope(q[:, base + LANES:base + 2 * LANES], cosm, sinm)
        qm_ref[h, :, LANES:2 * LANES] = (qr * s_mla).astype(BF16)

    ckvn = _rms(ckv_ref[...].astype(F32), gkv_ref[...]).astype(BF16)
    kn = jnp.dot(ckvn, wk_ref[...], preferred_element_type=F32)
    vv = jnp.dot(ckvn, wv_ref[...], preferred_element_type=F32)
    kr = _rope(kr_ref[...].astype(F32), cosm, sinm).astype(BF16)
    for h in range(MLA_HEADS):
        km_ref[h, :, 0:LANES] = kn[:, h * LANES:(h + 1) * LANES].astype(BF16)
        km_ref[h, :, LANES:2 * LANES] = kr
        vm_ref[h] = vv[:, h * LANES:(h + 1) * LANES].astype(BF16)

    for hc in range(2 * DIFF_HEADS):
        xq = dq_ref[:, hc * LANES:(hc + 1) * LANES].astype(F32)
        qd_ref[hc] = (_rope(xq, cosd, sind) * s_diff).astype(BF16)
        xk = dk_ref[:, hc * LANES:(hc + 1) * LANES].astype(F32)
        kd_ref[hc] = _rope(xk, cosd, sind).astype(BF16)


def _prep_call(proj, gq, gkv, wuq, wk, wv, cosd, sind, cosm, sinm):
    n = proj.shape[0]
    tm = _pick(n, (384, 256, 128))
    nh, nd = MLA_HEADS, 2 * DIFF_HEADS
    kern = functools.partial(_prep_kernel, s_mla=float((MLA_NOPE + MLA_ROPE) ** -0.5 * LOG2E),
                             s_diff=float(DIFF_QK ** -0.5 * LOG2E))
    row = lambda w, c: pl.BlockSpec((tm, w), lambda i, c=c: (i, c))
    const = lambda a: pl.BlockSpec(a.shape, lambda i: (0,) * a.ndim)
    tab = pl.BlockSpec((tm, LANES), lambda i: (i, 0))
    return pl.pallas_call(
        kern,
        grid=(n // tm,),
        in_specs=[row(512, _OFF_CQ // 512), row(512, _OFF_CKV // 512), row(1024, _OFF_DQ // 1024),
                  row(1024, _OFF_DK // 1024), row(LANES, _OFF_KR // LANES),
                  const(gq), const(gkv), const(wuq), const(wk), const(wv), tab, tab, tab, tab],
        out_specs=[pl.BlockSpec((nh, tm, MLA_QK_PAD), lambda i: (0, i, 0)),
                   pl.BlockSpec((nh, tm, MLA_QK_PAD), lambda i: (0, i, 0)),
                   pl.BlockSpec((nh, tm, MLA_V), lambda i: (0, i, 0)),
                   pl.BlockSpec((nd, tm, DIFF_QK), lambda i: (0, i, 0)),
                   pl.BlockSpec((nd, tm, DIFF_QK), lambda i: (0, i, 0))],
        out_shape=[jax.ShapeDtypeStruct((nh, n, MLA_QK_PAD), BF16),
                   jax.ShapeDtypeStruct((nh, n, MLA_QK_PAD), BF16),
                   jax.ShapeDtypeStruct((nh, n, MLA_V), BF16),
                   jax.ShapeDtypeStruct((nd, n, DIFF_QK), BF16),
                   jax.ShapeDtypeStruct((nd, n, DIFF_QK), BF16)],
        compiler_params=_cparams(("arbitrary",)),
        name="qkv_prep",
    )(proj, proj, proj, proj, proj, gq, gkv, wuq, wk, wv, cosd, sind, cosm, sinm)


def _softmax_pv(q, k_at, v_at, tk, n_chunks, dv):
    tq = q.shape[0]
    m = jnp.full((tq, 1), -jnp.inf, F32)
    l = jnp.zeros((tq, 1), F32)
    acc = jnp.zeros((tq, dv), F32)
    for c in range(n_chunks):
        off = c * tk
        s = lax.dot_general(q, k_at(off), (((1,), (1,)), ((), ())), preferred_element_type=F32)
        m_new = jnp.maximum(m, jnp.max(s, axis=-1, keepdims=True))
        alpha = jnp.exp2(m - m_new)
        p = jnp.exp2(s - m_new)
        l = alpha * l + jnp.sum(p, axis=-1, keepdims=True)
        acc = alpha * acc + jnp.dot(p.astype(BF16), v_at(off), preferred_element_type=F32)
        m = m_new
    return acc / l


def _mla_attn_kernel(q_ref, k_ref, v_ref, o_ref, *, tk, n_chunks):
    o = _softmax_pv(q_ref[0],
                    lambda off: k_ref[0, pl.ds(off, tk), :],
                    lambda off: v_ref[0, pl.ds(off, tk), :],
                    tk, n_chunks, MLA_V)
    o_ref[...] = o.astype(o_ref.dtype)


def _mla_attn_call(qm, km, vm, q_row0, n_q, kv_row0, n_kv):
    tq = _pick(n_q, (512, 256, 128))
    tk = _pick(n_kv, (2816, 768, 512, 256, 128))
    assert q_row0 % tq == 0 and kv_row0 % n_kv == 0
    qb, kb = q_row0 // tq, kv_row0 // n_kv
    kern = functools.partial(_mla_attn_kernel, tk=tk, n_chunks=n_kv // tk)
    return pl.pallas_call(
        kern,
        grid=(MLA_HEADS, n_q // tq),
        in_specs=[pl.BlockSpec((1, tq, MLA_QK_PAD), lambda h, i: (h, i + qb, 0)),
                  pl.BlockSpec((1, n_kv, MLA_QK_PAD), lambda h, i: (h, kb, 0)),
                  pl.BlockSpec((1, n_kv, MLA_V), lambda h, i: (h, kb, 0))],
        out_specs=pl.BlockSpec((tq, MLA_V), lambda h, i: (i, h)),
        out_shape=jax.ShapeDtypeStruct((n_q, MLA_HEADS * MLA_V), BF16),
        compiler_params=_cparams(("arbitrary", "arbitrary")),
        name="mla_attention",
    )(qm, km, vm)


def _diff_attn_kernel(q_ref, k_ref, v_ref, lam_ref, gs_ref, o_ref, *, tk, n_chunks):
    lam_init = lam_ref[4:5, 0:1]
    outs = []
    for comp in range(2):
        outs.append(_softmax_pv(q_ref[comp],
                                lambda off, comp=comp: k_ref[comp, pl.ds(off, tk), :],
                                lambda off: v_ref[pl.ds(off, tk), :],
                                tk, n_chunks, DIFF_V))
    lf = lam_ref[...]
    lam_full = (jnp.exp(jnp.sum(lf[0:1] * lf[1:2], axis=-1, keepdims=True))
                - jnp.exp(jnp.sum(lf[2:3] * lf[3:4], axis=-1, keepdims=True)) + lam_init)
    d = outs[0] - lam_full * outs[1]
    o_ref[...] = (_rms(d, gs_ref[...]) * (1.0 - lam_init)).astype(o_ref.dtype)


def _diff_attn_call(qd, kd, proj, lam, gsub, lam_init, q_row0, n_q, kv_row0, n_kv):
    tq = _pick(n_q, (512, 256, 128))
    tk = _pick(n_kv, (2816, 768, 512, 256, 128))
    assert q_row0 % tq == 0 and kv_row0 % n_kv == 0
    qb, kb = q_row0 // tq, kv_row0 // n_kv
    vcol = _OFF_DV // DIFF_V
    kern = functools.partial(_diff_attn_kernel, tk=tk, n_chunks=n_kv // tk)
    in_specs = [pl.BlockSpec((2, tq, DIFF_QK), lambda h, i: (h, i + qb, 0)),
                pl.BlockSpec((2, n_kv, DIFF_QK), lambda h, i: (h, kb, 0)),
                pl.BlockSpec((n_kv, DIFF_V), lambda h, i: (kb, vcol + h)),
                pl.BlockSpec((8, DIFF_QK), lambda h, i: (0, 0)),
                pl.BlockSpec((1, DIFF_V), lambda h, i: (0, 0))]
    lam8 = jnp.concatenate([lam, jnp.full((4, DIFF_QK), lam_init, F32)], axis=0)
    return pl.pallas_call(
        kern,
        grid=(DIFF_HEADS, n_q // tq),
        in_specs=in_specs,
        out_specs=pl.BlockSpec((tq, DIFF_V), lambda h, i: (i, h)),
        out_shape=jax.ShapeDtypeStruct((n_q, DIFF_HEADS * DIFF_V), BF16),
        compiler_params=_cparams(("arbitrary", "arbitrary")),
        name="diff_attention",
    )(qd, kd, proj, lam8, gsub.reshape(1, DIFF_V))


def _post_kernel(*refs, tm, n_lat, with_ctx):
    if with_ctx:
        (om_ref, od_ref, omc_ref, odc_ref, ga_ref, gb_ref, x_ref, wpm_ref, wpd_ref, wo_ref, g_ref, gate_ref,
         o_ref) = refs
    else:
        om_ref, od_ref, ga_ref, gb_ref, x_ref, wpm_ref, wpd_ref, wo_ref, g_ref, gate_ref, o_ref = refs
    i = pl.program_id(0)
    om, od, gate = om_ref[...], od_ref[...], gate_ref[0:1, :]
    if with_ctx:
        is_ctx = i * tm >= n_lat
        om = jnp.where(is_ctx, omc_ref[...], om)
        od = jnp.where(is_ctx, odc_ref[...], od)
        gate = jnp.where(is_ctx, gate_ref[1:2, :], gate)
    a = jnp.dot(om, wpm_ref[...], preferred_element_type=F32)
    b = jnp.dot(od, wpd_ref[...], preferred_element_type=F32)
    y = _sigmoid(ga_ref[...].astype(F32)) * a + _sigmoid(gb_ref[...].astype(F32)) * b
    z = jnp.dot(y.astype(BF16), wo_ref[...], preferred_element_type=F32)
    o_ref[...] = x_ref[...] + gate * _rms(z, g_ref[...])


def _post_call(o_mla, o_d, o_mla_ctx, o_d_ctx, proj, xs, wpm, wpd, wo, g, gate, n_lat):
    d = xs.shape[1]
    tm = 256
    with_ctx = o_mla_ctx is not None
    n_rows = n_lat + (o_mla_ctx.shape[0] if with_ctx else 0)
    assert n_rows % tm == 0 and n_lat % tm == 0
    nb_lat = n_lat // tm
    kern = functools.partial(_post_kernel, tm=tm, n_lat=n_lat, with_ctx=with_ctx)
    const = lambda a: pl.BlockSpec(a.shape, lambda i: (0,) * a.ndim, pipeline_mode=pl.Buffered(1))
    lat = lambda a: pl.BlockSpec((tm, a.shape[1]), lambda i: (jnp.minimum(i, nb_lat - 1), 0))
    cxs = lambda a: pl.BlockSpec((tm, a.shape[1]), lambda i: (jnp.maximum(i - nb_lat, 0), 0))
    in_specs = [lat(o_mla), lat(o_d)]
    args = [o_mla, o_d]
    if with_ctx:
        in_specs += [cxs(o_mla_ctx), cxs(o_d_ctx)]
        args += [o_mla_ctx, o_d_ctx]
    in_specs += [pl.BlockSpec((tm, d), lambda i: (i, _OFF_GA // D_MODEL)),
                 pl.BlockSpec((tm, d), lambda i: (i, _OFF_GB // D_MODEL)),
                 pl.BlockSpec((tm, d), lambda i: (i, 0)),
                 const(wpm), const(wpd), const(wo),
                 pl.BlockSpec((1, d), lambda i: (0, 0)),
                 pl.BlockSpec((8, d), lambda i: (0, 0))]
    args += [proj, proj, xs, wpm, wpd, wo, g.reshape(1, d), gate]
    return pl.pallas_call(
        kern,
        grid=(n_rows // tm,),
        in_specs=in_specs,
        out_specs=pl.BlockSpec((tm, d), lambda i: (i, 0)),
        out_shape=jax.ShapeDtypeStruct((n_rows, d), F32),
        compiler_params=_cparams(("arbitrary",)),
        name="merge_out_proj",
    )(*args)


def _router_kernel(x_ref, g_ref, mod_ref, wr_ref, h_ref, route_ref, cnt_ref, carry_ref, *, tm, n_rows, n_lat):
    i = pl.program_id(0)

    @pl.when(i == 0)
    def _():
        carry_ref[...] = jnp.zeros_like(carry_ref)

    if n_rows > n_lat:
        is_ctx = i * tm >= n_lat
        scale = jnp.where(is_ctx, mod_ref[3:4, :], mod_ref[0:1, :])
        shift = jnp.where(is_ctx, mod_ref[4:5, :], mod_ref[1:2, :])
    else:
        scale, shift = mod_ref[0:1, :], mod_ref[1:2, :]
    h = _rms(x_ref[...], g_ref[...]) * (1.0 + scale) + shift
    h_ref[...] = h
    logits = jnp.dot(h, wr_ref[...], preferred_element_type=F32, precision=lax.Precision.HIGHEST)
    lane = lax.broadcasted_iota(jnp.int32, logits.shape, 1).astype(F32)
    lg = jnp.where(lane < N_EXPERTS, logits, -jnp.inf)
    m1 = jnp.max(lg, axis=-1, keepdims=True)
    i1 = jnp.min(jnp.where(lg == m1, lane, float(LANES)), axis=-1, keepdims=True)
    lg2 = jnp.where(lane == i1, -jnp.inf, lg)
    m2 = jnp.max(lg2, axis=-1, keepdims=True)
    i2 = jnp.min(jnp.where(lg2 == m2, lane, float(LANES)), axis=-1, keepdims=True)
    e = jnp.exp(m2 - m1)
    den = 1.0 + e

    member = jnp.where(jnp.logical_or(lane == i1, lane == i2), 1.0, 0.0)
    rr = lax.broadcasted_iota(jnp.int32, (tm, tm), 0)
    cc = lax.broadcasted_iota(jnp.int32, (tm, tm), 1)
    lower = jnp.where(rr > cc, 1.0, 0.0).astype(BF16)
    rank = jnp.dot(lower, member.astype(BF16), preferred_element_type=F32) + carry_ref[0:1, :]
    r1 = jnp.sum(jnp.where(lane == i1, rank, 0.0), axis=-1, keepdims=True)
    r2 = jnp.sum(jnp.where(lane == i2, rank, 0.0), axis=-1, keepdims=True)
    total = carry_ref[0:1, :] + jnp.sum(member, axis=0, keepdims=True)
    carry_ref[...] = jnp.broadcast_to(total, carry_ref.shape)
    cnt_ref[...] = jnp.broadcast_to(total, cnt_ref.shape)

    cols = (i1, i2, 1.0 / den, e / den, r1, r2)
    route = jnp.zeros_like(logits)
    for j, v in enumerate(cols):
        route = jnp.where(lane == float(j), v, route)
    route_ref[...] = route


def _router_call(xs, g, mod, wr_pad, n_rows, n_lat):
    d = xs.shape[1]
    tm = 256
    assert n_rows % tm == 0 and n_lat % tm == 0
    kern = functools.partial(_router_kernel, tm=tm, n_rows=n_rows, n_lat=n_lat)
    return pl.pallas_call(
        kern,
        grid=(n_rows // tm,),
        in_specs=[pl.BlockSpec((tm, d), lambda i: (i, 0)),
                  pl.BlockSpec((1, d), lambda i: (0, 0)),
                  pl.BlockSpec((8, d), lambda i: (0, 0)),
                  pl.BlockSpec((d, LANES), lambda i: (0, 0))],
        out_specs=[pl.BlockSpec((tm, d), lambda i: (i, 0)),
                   pl.BlockSpec((tm, LANES), lambda i: (i, 0)),
                   pl.BlockSpec((8, LANES), lambda i: (0, 0))],
        out_shape=[jax.ShapeDtypeStruct((n_rows, d), F32),
                   jax.ShapeDtypeStruct((n_rows, LANES), F32),
                   jax.ShapeDtypeStruct((8, LANES), F32)],
        scratch_shapes=[pltpu.VMEM((8, LANES), F32)],
        compiler_params=_cparams(("arbitrary",)),
        name="moe_router",
    )(xs, g.reshape(1, d), mod, wr_pad)


MOE_TB = 1024
MOE_SUB = 512
MOE_TF = 256
DMA_UNROLL = 8


def _dispatch_kernel(d1_ref, d2_ref, h_ref, xs_in_ref, xs_ref, sem, *, tm):
    del xs_in_ref

    def issue(t, carry):
        src = h_ref.at[pl.ds(t, 1), :]
        pltpu.make_async_copy(src, xs_ref.at[pl.ds(d1_ref[0, 0, t], 1), :], sem).start()
        pltpu.make_async_copy(src, xs_ref.at[pl.ds(d2_ref[0, 0, t], 1), :], sem).start()
        return carry

    lax.fori_loop(0, tm, issue, 0, unroll=DMA_UNROLL)

    def drain(t, carry):
        cp = pltpu.make_async_copy(h_ref.at[pl.ds(0, 1), :], xs_ref.at[pl.ds(0, 1), :], sem)
        cp.wait()
        cp.wait()
        return carry

    lax.fori_loop(0, tm, drain, 0, unroll=DMA_UNROLL)


def _dispatch_call(h, dest1, dest2, n_slots):
    n, d = h.shape
    tm = 256
    kern = functools.partial(_dispatch_kernel, tm=tm)
    idx = pl.BlockSpec((1, 1, tm), lambda i: (i, 0, 0), memory_space=pltpu.SMEM)
    anyspec = pl.BlockSpec(memory_space=pl.ANY)
    return pl.pallas_call(
        kern,
        grid=(n // tm,),
        in_specs=[idx, idx, pl.BlockSpec((tm, d), lambda i: (i, 0)), anyspec],
        out_specs=anyspec,
        out_shape=jax.ShapeDtypeStruct((n_slots, d), F32),
        scratch_shapes=[pltpu.SemaphoreType.DMA(())],
        input_output_aliases={3: 0},
        compiler_params=_cparams(("arbitrary",)),
        name="moe_dispatch",
    )(dest1.reshape(n // tm, 1, tm), dest2.reshape(n // tm, 1, tm), h, jnp.zeros((n_slots, d), F32))


def _grouped_ffn_kernel(be_ref, nv_ref, rv_ref, x_ref, w1_ref, w3_ref, w2_ref, o_ref, xb_ref):
    del be_ref, nv_ref
    b = pl.program_id(0)
    k = pl.program_id(1)
    rows = rv_ref[b]
    w1 = w1_ref[0].astype(BF16)
    w3 = w3_ref[0].astype(BF16)
    w2 = w2_ref[0].astype(BF16)

    for r0 in range(0, MOE_TB, MOE_SUB):
        live = rows > r0

        @pl.when(jnp.logical_and(live, k == 0))
        def _():
            xb_ref[r0:r0 + MOE_SUB, :] = x_ref[r0:r0 + MOE_SUB, :].astype(BF16)

        @pl.when(live)
        def _():
            xb = xb_ref[r0:r0 + MOE_SUB, :]
            u1 = jnp.dot(xb, w1, preferred_element_type=F32)
            u3 = jnp.dot(xb, w3, preferred_element_type=F32)
            u = (u1 * _sigmoid(u1)) * u3
            part = jnp.dot(u.astype(BF16), w2, preferred_element_type=F32)

            @pl.when(k == 0)
            def _():
                o_ref[r0:r0 + MOE_SUB, :] = part

            @pl.when(k > 0)
            def _():
                o_ref[r0:r0 + MOE_SUB, :] += part

        @pl.when(jnp.logical_and(jnp.logical_not(live), k == 0))
        def _():
            o_ref[r0:r0 + MOE_SUB, :] = jnp.zeros((MOE_SUB, o_ref.shape[1]), F32)


def _grouped_ffn_call(xs, w1, w3, w2, block_expert, n_valid, rows_valid):
    n_slots, d = xs.shape
    f = w1.shape[2]
    tb, tf = MOE_TB, MOE_TF
    n_k = f // tf
    assert n_slots % tb == 0 and f % tf == 0

    def blk(b, nv):
        return jnp.minimum(b, nv[0] - 1)

    def kk(b, k, nv):
        return jnp.where(b < nv[0], k, n_k - 1)

    grid_spec = pltpu.PrefetchScalarGridSpec(
        num_scalar_prefetch=3,
        grid=(n_slots // tb, n_k),
        in_specs=[pl.BlockSpec((tb, d), lambda b, k, be, nv, rv: (blk(b, nv), 0)),
                  pl.BlockSpec((1, d, tf), lambda b, k, be, nv, rv: (be[blk(b, nv)], 0, kk(b, k, nv))),
                  pl.BlockSpec((1, d, tf), lambda b, k, be, nv, rv: (be[blk(b, nv)], 0, kk(b, k, nv))),
                  pl.BlockSpec((1, tf, d), lambda b, k, be, nv, rv: (be[blk(b, nv)], kk(b, k, nv), 0))],
        out_specs=pl.BlockSpec((tb, d), lambda b, k, be, nv, rv: (b, 0)),
        scratch_shapes=[pltpu.VMEM((tb, d), BF16)])
    return pl.pallas_call(
        _grouped_ffn_kernel,
        grid_spec=grid_spec,
        out_shape=jax.ShapeDtypeStruct((n_slots, d), F32),
        compiler_params=_cparams(("arbitrary", "arbitrary")),
        name="moe_grouped_ffn",
    )(block_expert, n_valid, rows_valid, xs, w1, w3, w2)


def _combine_kernel(d1_ref, d2_ref, route_ref, x_ref, g3_ref, mod_ref, y_ref, o_ref, ya_ref, yb_ref, sem,
                    *, tm, n_rows, n_lat):
    i = pl.program_id(0)

    def issue(t, carry):
        pltpu.make_async_copy(y_ref.at[pl.ds(d1_ref[0, 0, t], 1), :], ya_ref.at[pl.ds(t, 1), :], sem).start()
        pltpu.make_async_copy(y_ref.at[pl.ds(d2_ref[0, 0, t], 1), :], yb_ref.at[pl.ds(t, 1), :], sem).start()
        return carry

    lax.fori_loop(0, tm, issue, 0, unroll=DMA_UNROLL)

    def drain(t, carry):
        cp = pltpu.make_async_copy(y_ref.at[pl.ds(0, 1), :], ya_ref.at[pl.ds(0, 1), :], sem)
        cp.wait()
        cp.wait()
        return carry

    lax.fori_loop(0, tm, drain, 0, unroll=DMA_UNROLL)

    r = route_ref[...]
    f = r[:, 2:3] * ya_ref[...] + r[:, 3:4] * yb_ref[...]
    if n_rows > n_lat:
        gate = jnp.where(i * tm >= n_lat, mod_ref[5:6, :], mod_ref[2:3, :])
    else:
        gate = mod_ref[2:3, :]
    o_ref[...] = x_ref[...] + gate * _rms(f, g3_ref[...])


def _combine_call(y, dest1, dest2, route, xs, g3, mod, n_rows, n_lat):
    d = xs.shape[1]
    tm = 256
    kern = functools.partial(_combine_kernel, tm=tm, n_rows=n_rows, n_lat=n_lat)
    idx = pl.BlockSpec((1, 1, tm), lambda i: (i, 0, 0), memory_space=pltpu.SMEM)
    return pl.pallas_call(
        kern,
        grid=(n_rows // tm,),
        in_specs=[idx, idx,
                  pl.BlockSpec((tm, LANES), lambda i: (i, 0)),
                  pl.BlockSpec((tm, d), lambda i: (i, 0)),
                  pl.BlockSpec((1, d), lambda i: (0, 0)),
                  pl.BlockSpec((8, d), lambda i: (0, 0)),
                  pl.BlockSpec(memory_space=pl.ANY)],
        out_specs=pl.BlockSpec((tm, d), lambda i: (i, 0)),
        out_shape=jax.ShapeDtypeStruct((n_rows, d), F32),
        scratch_shapes=[pltpu.VMEM((tm, d), F32), pltpu.VMEM((tm, d), F32), pltpu.SemaphoreType.DMA(())],
        compiler_params=_cparams(("arbitrary",)),
        name="moe_combine",
    )(dest1.reshape(n_rows // tm, 1, tm), dest2.reshape(n_rows // tm, 1, tm), route, xs, g3.reshape(1, d), mod, y)


def _moe_call(xs, g2, mod, wr_pad, w1, w3, w2, g3, n_rows, n_lat):
    h, route, cnt = _router_call(xs, g2, mod, wr_pad, n_rows, n_lat)
    tb = MOE_TB
    n_slots = -(-(2 * n_rows + N_EXPERTS * (tb - 1)) // tb) * tb
    counts = cnt[0, :N_EXPERTS].astype(jnp.int32)
    padded = (counts + tb - 1) // tb * tb
    ends = jnp.cumsum(padded)
    offs = ends - padded
    e1, e2 = route[:, 0].astype(jnp.int32), route[:, 1].astype(jnp.int32)
    dest1 = offs[e1] + route[:, 4].astype(jnp.int32)
    dest2 = offs[e2] + route[:, 5].astype(jnp.int32)
    blk_start = jnp.arange(n_slots // tb, dtype=jnp.int32) * tb
    block_expert = jnp.minimum(jnp.sum(blk_start[:, None] >= ends[None, :], axis=1), N_EXPERTS - 1)
    n_valid = (ends[-1:] // tb).astype(jnp.int32)
    rows_valid = jnp.clip((offs + counts)[block_expert] - blk_start, 0, tb).astype(jnp.int32)
    rows_valid = jnp.where(blk_start < ends[-1], rows_valid, 0)
    xg = _dispatch_call(h, dest1, dest2, n_slots)
    y = _grouped_ffn_call(xg, w1, w3, w2, block_expert.astype(jnp.int32), n_valid, rows_valid)
    return _combine_call(y, dest1, dest2, route, xs, g3, mod, n_rows, n_lat)


def _ffn_kernel(x_ref, g2_ref, mod_ref, w1_ref, w3_ref, w2_ref, g3_ref, o_ref, h_ref, acc_ref,
                *, tm, n_rows, n_lat, n_k):
    i = pl.program_id(0)
    k = pl.program_id(1)

    @pl.when(k == 0)
    def _():
        def fill(r0, r1, is_ctx):
            r = 3 if is_ctx else 0
            h = _rms(x_ref[r0:r1, :], g2_ref[...]) * (1.0 + mod_ref[r:r + 1, :]) + mod_ref[r + 1:r + 2, :]
            h_ref[r0:r1, :] = h.astype(BF16)
        _row_groups(i, tm, n_rows, n_lat, fill)
        acc_ref[...] = jnp.zeros_like(acc_ref)

    h = h_ref[...]
    u1 = jnp.dot(h, w1_ref[...], preferred_element_type=F32)
    u3 = jnp.dot(h, w3_ref[...], preferred_element_type=F32)
    u = (u1 * _sigmoid(u1)) * u3
    acc_ref[...] += jnp.dot(u.astype(BF16), w2_ref[...], preferred_element_type=F32)

    @pl.when(k == n_k - 1)
    def _():
        def fin(r0, r1, is_ctx):
            r = 5 if is_ctx else 2
            fn = _rms(acc_ref[r0:r1, :], g3_ref[...])
            o_ref[r0:r1, :] = x_ref[r0:r1, :] + mod_ref[r:r + 1, :] * fn
        _row_groups(i, tm, n_rows, n_lat, fin)


def _ffn_call(xs, g2, mod, w1, w3, w2, g3, n_rows, n_lat):
    d = xs.shape[1]
    f = w1.shape[1]
    tm = _pick(n_rows, (768, 512, 256))
    tf = _pick(f, (512, 256, 128))
    n_k = f // tf
    kern = functools.partial(_ffn_kernel, tm=tm, n_rows=n_rows, n_lat=n_lat, n_k=n_k)
    return pl.pallas_call(
        kern,
        grid=(n_rows // tm, n_k),
        in_specs=[pl.BlockSpec((tm, d), lambda i, k: (i, 0)),
                  pl.BlockSpec((1, d), lambda i, k: (0, 0)),
                  pl.BlockSpec((8, d), lambda i, k: (0, 0)),
                  pl.BlockSpec((d, tf), lambda i, k: (0, k)),
                  pl.BlockSpec((d, tf), lambda i, k: (0, k)),
                  pl.BlockSpec((tf, d), lambda i, k: (k, 0)),
                  pl.BlockSpec((1, d), lambda i, k: (0, 0))],
        out_specs=pl.BlockSpec((tm, d), lambda i, k: (i, 0)),
        out_shape=jax.ShapeDtypeStruct((n_rows, d), F32),
        scratch_shapes=[pltpu.VMEM((tm, d), BF16), pltpu.VMEM((tm, d), F32)],
        compiler_params=_cparams(("arbitrary", "arbitrary")),
        name="dense_ffn",
    )(xs, g2.reshape(1, d), mod, w1, w3, w2, g3.reshape(1, d))


def _diff_perm_cols(w, base):
    parts = []
    for b in range(8):
        o = base + b * LANES
        parts += [w[..., o:o + 32], w[..., o + 64:o + 96], w[..., o + 32:o + 64], w[..., o + 96:o + 128]]
    return parts


def _mla_rope_slot(w, base):
    z = jnp.zeros(w.shape[:-1] + (32,), w.dtype)
    return [w[..., base:base + 16], w[..., base + 32:base + 48], z,
            w[..., base + 16:base + 32], w[..., base + 48:base + 64], z]


def _prep_w_in(w_in):
    o_kr, o_dq, o_dk, o_dv, o_ga = 1024, 1088, 2112, 3136, 4160
    parts = [w_in[..., 0:1024]]
    parts += _diff_perm_cols(w_in, o_dq)
    parts += _diff_perm_cols(w_in, o_dk)
    parts += [w_in[..., o_dv:o_ga], w_in[..., o_ga:]]
    parts += _mla_rope_slot(w_in, o_kr)
    parts += [jnp.zeros(w_in.shape[:-1] + (LANES,), w_in.dtype)]
    return jnp.concatenate(parts, axis=-1).astype(BF16)


def _prep_w_uq(w_uq):
    parts = []
    per = MLA_NOPE + MLA_ROPE
    for h in range(MLA_HEADS):
        parts.append(w_uq[..., h * per:h * per + MLA_NOPE])
        parts += _mla_rope_slot(w_uq, h * per + MLA_NOPE)
    return jnp.concatenate(parts, axis=-1).astype(BF16)


def _prep_w_ukv(w_ukv):
    per = MLA_NOPE + MLA_V
    wk = jnp.concatenate([w_ukv[..., h * per:h * per + MLA_NOPE] for h in range(MLA_HEADS)], axis=-1)
    wv = jnp.concatenate([w_ukv[..., h * per + MLA_NOPE:(h + 1) * per] for h in range(MLA_HEADS)], axis=-1)
    return wk.astype(BF16), wv.astype(BF16)


def _rope_tables(n_lat, n_ctx):
    t = jnp.arange(n_lat, dtype=jnp.int32)
    row = (t // GRID_W).astype(F32)[:, None]
    col = (t % GRID_W).astype(F32)[:, None]

    def table(half, pad):
        freqs = ROPE_BASE ** (-jnp.arange(half, dtype=F32) / half)
        a = jnp.concatenate([row * freqs, col * freqs], axis=-1)
        z = jnp.zeros((n_lat, pad), F32)
        cos = jnp.concatenate([jnp.cos(a), 1.0 + z, jnp.cos(a), 1.0 + z], axis=-1)
        sin = jnp.concatenate([-jnp.sin(a), z, jnp.sin(a), z], axis=-1)
        cos = jnp.concatenate([cos, jnp.ones((n_ctx, LANES), F32)], axis=0)
        sin = jnp.concatenate([sin, jnp.zeros((n_ctx, LANES), F32)], axis=0)
        return cos, sin

    cosd, sind = table(DIFF_QK // 4, 0)
    cosm, sinm = table(MLA_ROPE // 4, 32)
    return cosd, sind, cosm, sinm


def kernel(x, c, ctx, c_ctx, w_mod, b_mod, g_norm, w_in, w_uq, g_qn, w_ukv, g_kvn, lam, g_sub, w_po_mla,
           w_po_diff, w_out, w1_dense, w3_dense, w2_dense, w_router, w1_moe, w3_moe, w2_moe):
    b, s, d = x.shape
    n_ctx = ctx.shape[1]
    depth = w_mod.shape[0]
    assert b == 1 and d == D_MODEL and s % GRID_W == 0
    n_tok = s + n_ctx

    xs = jnp.concatenate([x[0], ctx[0]], axis=0)
    c8 = jnp.concatenate([c, c_ctx[None, :], jnp.zeros((6, d), F32)], axis=0)
    mods = _mod_call(c8, w_mod, b_mod)

    w_in_p = _prep_w_in(w_in)
    w_uq_p = _prep_w_uq(w_uq)
    w_k_p, w_v_p = _prep_w_ukv(w_ukv)
    w_pm, w_pd, w_o = w_po_mla.astype(BF16), w_po_diff.astype(BF16), w_out.astype(BF16)
    w1d, w3d, w2d = w1_dense.astype(BF16), w3_dense.astype(BF16), w2_dense.astype(BF16)
    wr_pad = jnp.pad(w_router, ((0, 0), (0, 0), (0, LANES - N_EXPERTS)))
    cosd, sind, cosm, sinm = _rope_tables(s, n_ctx)
    zrow = jnp.zeros((d,), F32)

    for layer in range(depth):
        need_ctx = layer < depth - 1
        lat, cx = mods[layer, 0], mods[layer, 1]
        seg = lambda v, k: v[k * d:(k + 1) * d]
        gn = g_norm[layer]

        mod1 = jnp.stack([seg(lat, 1), seg(lat, 0), seg(cx, 1), seg(cx, 0), zrow, zrow, zrow, zrow])
        proj = _norm_matmul_call(xs, gn[0], mod1, w_in_p[layer], s)
        qm, km, vm, qd, kd = _prep_call(proj, g_qn[layer].reshape(1, -1), g_kvn[layer].reshape(1, -1),
                                        w_uq_p[layer], w_k_p[layer], w_v_p[layer], cosd, sind, cosm, sinm)
        lam_init = 0.8 - 0.6 * math.exp(-0.3 * layer)
        o_mla = _mla_attn_call(qm, km, vm, 0, s, 0, n_tok)
        o_d = _diff_attn_call(qd, kd, proj, lam[layer], g_sub[layer], lam_init, 0, s, 0, n_tok)
        o_mla_c = o_d_c = None
        if need_ctx:
            o_mla_c = _mla_attn_call(qm, km, vm, s, n_ctx, s, n_ctx)
            o_d_c = _diff_attn_call(qd, kd, proj, lam[layer], g_sub[layer], lam_init, s, n_ctx, s, n_ctx)
        n_rows = n_tok if need_ctx else s
        gate1 = jnp.stack([seg(lat, 2), seg(cx, 2), zrow, zrow, zrow, zrow, zrow, zrow])
        xs = _post_call(o_mla, o_d, o_mla_c, o_d_c, proj, xs, w_pm[layer], w_pd[layer], w_o[layer], gn[1],
                        gate1, s)

        mod2 = jnp.stack([seg(lat, 4), seg(lat, 3), seg(lat, 5), seg(cx, 4), seg(cx, 3), seg(cx, 5), zrow, zrow])
        i = layer // 2
        if layer % 2 == 0:
            xs = _ffn_call(xs, gn[2], mod2, w1d[i], w3d[i], w2d[i], gn[3], n_rows, s)
        else:
            xs = _moe_call(xs, gn[2], mod2, wr_pad[i], w1_moe[i], w3_moe[i], w2_moe[i], gn[3], n_rows, s)
    return xs[:s][None]
```

```python
import functools
import math

import jax
import jax.numpy as jnp
from jax import lax
from jax.experimental import pallas as pl
from jax.experimental.pallas import tpu as pltpu

F32 = jnp.float32
BF16 = jnp.bfloat16

D_MODEL = 2048
GRID_W = 64
MLA_HEADS = 8
MLA_Q_RANK = 512
MLA_KV_RANK = 512
MLA_NOPE = 128
MLA_ROPE = 64
MLA_V = 128
MLA_QK_PAD = 256
DIFF_HEADS = 4
DIFF_QK = 128
DIFF_V = 256
N_EXPERTS = 8
ROPE_BASE = 10000.0
NORM_EPS = 1e-6
LANES = 128
LOG2E = math.log2(math.e)

_OFF_CQ, _OFF_CKV, _OFF_DQ, _OFF_DK, _OFF_DV, _OFF_GA, _OFF_GB, _OFF_KR = (
    0, 512, 1024, 2048, 3072, 4096, 6144, 8192)
D_IN_PAD = 8448

VMEM_LIMIT = 56 * 1024 * 1024


def _pick(n, cands):
    for c in cands:
        if n % c == 0:
            return c
    raise ValueError(f"no tile for {n} in {cands}")


def _cparams(sem):
    return pltpu.CompilerParams(dimension_semantics=sem, vmem_limit_bytes=VMEM_LIMIT)


def _rms(xf, g):
    ms = jnp.mean(xf * xf, axis=-1, keepdims=True)
    return xf * lax.rsqrt(ms + NORM_EPS) * g


def _sigmoid(v):
    return 1.0 / (1.0 + jnp.exp(-v))


def _row_groups(i, tm, n_rows, n_lat, fn):
    if n_rows <= n_lat:
        fn(0, tm, False)
        return
    nb = n_rows // tm
    b, off = divmod(n_lat, tm)
    if b > 0:
        pl.when(i < b)(lambda: fn(0, tm, False))
    if off:
        def _split():
            fn(0, off, False)
            fn(off, tm, True)
        pl.when(i == b)(_split)
        if nb > b + 1:
            pl.when(i > b)(lambda: fn(0, tm, True))
    else:
        pl.when(i >= b)(lambda: fn(0, tm, True))


def _mod_kernel(c_ref, w_ref, b_ref, o_ref):
    cv = c_ref[...]
    s = cv * _sigmoid(cv)
    o_ref[0] = jnp.dot(s, w_ref[0], preferred_element_type=F32,
                       precision=lax.Precision.HIGHEST) + b_ref[0]


def _mod_call(c8, w_mod, b_mod):
    depth, d, n6 = w_mod.shape
    tn = _pick(n6, (1536, 1024, 512, 256, 128))
    return pl.pallas_call(
        _mod_kernel,
        grid=(depth, n6 // tn),
        in_specs=[pl.BlockSpec((8, d), lambda l, j: (0, 0)),
                  pl.BlockSpec((1, d, tn), lambda l, j: (l, 0, j)),
                  pl.BlockSpec((1, 1, tn), lambda l, j: (l, 0, j))],
        out_specs=pl.BlockSpec((1, 8, tn), lambda l, j: (l, 0, j)),
        out_shape=jax.ShapeDtypeStruct((depth, 8, n6), F32),
        compiler_params=_cparams(("arbitrary", "arbitrary")),
        name="adaln_mod",
    )(c8, w_mod, b_mod.reshape(depth, 1, n6))


def _norm_matmul_kernel(x_ref, g_ref, mod_ref, w_ref, o_ref, h_ref, *, tm, n_rows, n_lat):
    i = pl.program_id(0)
    j = pl.program_id(1)

    @pl.when(j == 0)
    def _():
        def fill(r0, r1, is_ctx):
            r = 2 if is_ctx else 0
            h = _rms(x_ref[r0:r1, :], g_ref[...]) * (1.0 + mod_ref[r:r + 1, :]) + mod_ref[r + 1:r + 2, :]
            h_ref[r0:r1, :] = h.astype(BF16)
        _row_groups(i, tm, n_rows, n_lat, fill)

    o_ref[...] = jnp.dot(h_ref[...], w_ref[0], preferred_element_type=F32).astype(o_ref.dtype)


def _norm_matmul_call(xs, g, mod, w, layer, n_lat):
    n_rows, d = xs.shape
    n_out = w.shape[2]
    tm = _pick(n_rows, (768, 512, 256))
    tn = _pick(n_out, (1408, 1024, 768, 512, 256))
    kern = functools.partial(_norm_matmul_kernel, tm=tm, n_rows=n_rows, n_lat=n_lat)
    return pl.pallas_call(
        kern,
        grid=(n_rows // tm, n_out // tn),
        in_specs=[pl.BlockSpec((tm, d), lambda i, j: (i, 0)),
                  pl.BlockSpec((1, d), lambda i, j: (0, 0)),
                  pl.BlockSpec((8, d), lambda i, j: (0, 0)),
                  pl.BlockSpec((1, d, tn), lambda i, j: (layer, 0, j))],
        out_specs=pl.BlockSpec((tm, tn), lambda i, j: (i, j)),
        out_shape=jax.ShapeDtypeStruct((n_rows, n_out), BF16),
        scratch_shapes=[pltpu.VMEM((tm, d), BF16)],
        compiler_params=_cparams(("arbitrary", "arbitrary")),
        name="norm_in_proj",
    )(xs, g.reshape(1, d), mod, w)


def _rope(xf, tab, half):
    cos, sin_lo, sin_hi = tab
    return xf * cos + pltpu.roll(xf, LANES - half, 1) * sin_lo + pltpu.roll(xf, half, 1) * sin_hi


def _prep_kernel(cq_ref, ckv_ref, dq_ref, dk_ref, kr_ref, gq_ref, gkv_ref, wuq_ref, wk_ref, wv_ref,
                 cosd_ref, sld_ref, shd_ref, cosm_ref, slm_ref, shm_ref,
                 qm_ref, km_ref, vm_ref, qd_ref, kd_ref, *, s_mla, s_diff):
    tabm = (cosm_ref[...], slm_ref[...], shm_ref[...])
    tabd = (cosd_ref[...], sld_ref[...], shd_ref[...])
    half_m, half_d = MLA_ROPE // 4, DIFF_QK // 4

    cqn = _rms(cq_ref[...].astype(F32), gq_ref[...]).astype(BF16)
    q = jnp.dot(cqn, wuq_ref[0], preferred_element_type=F32)
    for h in range(MLA_HEADS):
        base = h * MLA_QK_PAD
        qm_ref[h, :, 0:LANES] = (q[:, base:base + LANES] * s_mla).astype(BF16)
        qr = _rope(q[:, base + LANES:base + 2 * LANES], tabm, half_m)
        qm_ref[h, :, LANES:2 * LANES] = (qr * s_mla).astype(BF16)

    ckvn = _rms(ckv_ref[...].astype(F32), gkv_ref[...]).astype(BF16)
    kn = jnp.dot(ckvn, wk_ref[0], preferred_element_type=F32)
    vv = jnp.dot(ckvn, wv_ref[0], preferred_element_type=F32)
    kr = _rope(kr_ref[...].astype(F32), tabm, half_m).astype(BF16)
    for h in range(MLA_HEADS):
        km_ref[h, :, 0:LANES] = kn[:, h * LANES:(h + 1) * LANES].astype(BF16)
        km_ref[h, :, LANES:2 * LANES] = kr
        vm_ref[h] = vv[:, h * LANES:(h + 1) * LANES].astype(BF16)

    for hc in range(2 * DIFF_HEADS):
        xq = dq_ref[:, hc * LANES:(hc + 1) * LANES].astype(F32)
        qd_ref[hc] = (_rope(xq, tabd, half_d) * s_diff).astype(BF16)
        xk = dk_ref[:, hc * LANES:(hc + 1) * LANES].astype(F32)
        kd_ref[hc] = _rope(xk, tabd, half_d).astype(BF16)


def _prep_call(proj, gq, gkv, wuq, wk, wv, layer, tabs):
    n = proj.shape[0]
    tm = _pick(n, (384, 256, 128))
    nh, nd = MLA_HEADS, 2 * DIFF_HEADS
    kern = functools.partial(_prep_kernel, s_mla=float((MLA_NOPE + MLA_ROPE) ** -0.5 * LOG2E),
                             s_diff=float(DIFF_QK ** -0.5 * LOG2E))
    row = lambda w, c: pl.BlockSpec((tm, w), lambda i, c=c: (i, c))
    const = lambda a: pl.BlockSpec(a.shape, lambda i: (0,) * a.ndim)
    wspec = lambda a: pl.BlockSpec((1,) + a.shape[1:], lambda i: (layer, 0, 0))
    tab = pl.BlockSpec((tm, LANES), lambda i: (i, 0))
    return pl.pallas_call(
        kern,
        grid=(n // tm,),
        in_specs=[row(512, _OFF_CQ // 512), row(512, _OFF_CKV // 512), row(1024, _OFF_DQ // 1024),
                  row(1024, _OFF_DK // 1024), row(LANES, _OFF_KR // LANES),
                  const(gq), const(gkv), wspec(wuq), wspec(wk), wspec(wv)] + [tab] * 6,
        out_specs=[pl.BlockSpec((nh, tm, MLA_QK_PAD), lambda i: (0, i, 0)),
                   pl.BlockSpec((nh, tm, MLA_QK_PAD), lambda i: (0, i, 0)),
                   pl.BlockSpec((nh, tm, MLA_V), lambda i: (0, i, 0)),
                   pl.BlockSpec((nd, tm, DIFF_QK), lambda i: (0, i, 0)),
                   pl.BlockSpec((nd, tm, DIFF_QK), lambda i: (0, i, 0))],
        out_shape=[jax.ShapeDtypeStruct((nh, n, MLA_QK_PAD), BF16),
                   jax.ShapeDtypeStruct((nh, n, MLA_QK_PAD), BF16),
                   jax.ShapeDtypeStruct((nh, n, MLA_V), BF16),
                   jax.ShapeDtypeStruct((nd, n, DIFF_QK), BF16),
                   jax.ShapeDtypeStruct((nd, n, DIFF_QK), BF16)],
        compiler_params=_cparams(("arbitrary",)),
        name="qkv_prep",
    )(proj, proj, proj, proj, proj, gq, gkv, wuq, wk, wv, *tabs)


def _softmax_pv(q, k_at, v_at, tk, n_chunks, dv):
    tq = q.shape[0]
    m = jnp.full((tq, 1), -jnp.inf, F32)
    l = jnp.zeros((tq, 1), F32)
    acc = jnp.zeros((tq, dv), F32)
    for c in range(n_chunks):
        off = c * tk
        s = lax.dot_general(q, k_at(off), (((1,), (1,)), ((), ())), preferred_element_type=F32)
        m_new = jnp.maximum(m, jnp.max(s, axis=-1, keepdims=True))
        alpha = jnp.exp2(m - m_new)
        p = jnp.exp2(s - m_new)
        l = alpha * l + jnp.sum(p, axis=-1, keepdims=True)
        acc = alpha * acc + jnp.dot(p.astype(BF16), v_at(off), preferred_element_type=F32)
        m = m_new
    return acc / l


def _mla_attn_kernel(q_ref, k_ref, v_ref, o_ref, *, tk, n_chunks):
    o = _softmax_pv(q_ref[0],
                    lambda off: k_ref[0, pl.ds(off, tk), :],
                    lambda off: v_ref[0, pl.ds(off, tk), :],
                    tk, n_chunks, MLA_V)
    o_ref[...] = o.astype(o_ref.dtype)


def _mla_attn_call(qm, km, vm, q_row0, n_q, kv_row0, n_kv):
    tq = _pick(n_q, (512, 256, 128))
    tk = _pick(n_kv, (2816, 768, 512, 256, 128))
    assert q_row0 % tq == 0 and kv_row0 % n_kv == 0
    qb, kb = q_row0 // tq, kv_row0 // n_kv
    kern = functools.partial(_mla_attn_kernel, tk=tk, n_chunks=n_kv // tk)
    return pl.pallas_call(
        kern,
        grid=(MLA_HEADS, n_q // tq),
        in_specs=[pl.BlockSpec((1, tq, MLA_QK_PAD), lambda h, i: (h, i + qb, 0)),
                  pl.BlockSpec((1, n_kv, MLA_QK_PAD), lambda h, i: (h, kb, 0)),
                  pl.BlockSpec((1, n_kv, MLA_V), lambda h, i: (h, kb, 0))],
        out_specs=pl.BlockSpec((tq, MLA_V), lambda h, i: (i, h)),
        out_shape=jax.ShapeDtypeStruct((n_q, MLA_HEADS * MLA_V), BF16),
        compiler_params=_cparams(("arbitrary", "arbitrary")),
        name="mla_attention",
    )(qm, km, vm)


def _diff_attn_kernel(q_ref, k_ref, v_ref, lam_ref, gs_ref, o_ref, *, tk, n_chunks):
    lam_init = lam_ref[4:5, 0:1]
    outs = []
    for comp in range(2):
        outs.append(_softmax_pv(q_ref[comp],
                                lambda off, comp=comp: k_ref[comp, pl.ds(off, tk), :],
                                lambda off: v_ref[pl.ds(off, tk), :],
                                tk, n_chunks, DIFF_V))
    lf = lam_ref[...]
    lam_full = (jnp.exp(jnp.sum(lf[0:1] * lf[1:2], axis=-1, keepdims=True))
                - jnp.exp(jnp.sum(lf[2:3] * lf[3:4], axis=-1, keepdims=True)) + lam_init)
    d = outs[0] - lam_full * outs[1]
    o_ref[...] = (_rms(d, gs_ref[...]) * (1.0 - lam_init)).astype(o_ref.dtype)


def _diff_attn_call(qd, kd, proj, lam, gsub, lam_init, q_row0, n_q, kv_row0, n_kv):
    tq = _pick(n_q, (512, 256, 128))
    tk = _pick(n_kv, (2816, 768, 512, 256, 128))
    assert q_row0 % tq == 0 and kv_row0 % n_kv == 0
    qb, kb = q_row0 // tq, kv_row0 // n_kv
    vcol = _OFF_DV // DIFF_V
    kern = functools.partial(_diff_attn_kernel, tk=tk, n_chunks=n_kv // tk)
    in_specs = [pl.BlockSpec((2, tq, DIFF_QK), lambda h, i: (h, i + qb, 0)),
                pl.BlockSpec((2, n_kv, DIFF_QK), lambda h, i: (h, kb, 0)),
                pl.BlockSpec((n_kv, DIFF_V), lambda h, i: (kb, vcol + h)),
                pl.BlockSpec((8, DIFF_QK), lambda h, i: (0, 0)),
                pl.BlockSpec((1, DIFF_V), lambda h, i: (0, 0))]
    lam8 = jnp.concatenate([lam, jnp.full((4, DIFF_QK), lam_init, F32)], axis=0)
    return pl.pallas_call(
        kern,
        grid=(DIFF_HEADS, n_q // tq),
        in_specs=in_specs,
        out_specs=pl.BlockSpec((tq, DIFF_V), lambda h, i: (i, h)),
        out_shape=jax.ShapeDtypeStruct((n_q, DIFF_HEADS * DIFF_V), BF16),
        compiler_params=_cparams(("arbitrary", "arbitrary")),
        name="diff_attention",
    )(qd, kd, proj, lam8, gsub.reshape(1, DIFF_V))


def _post_kernel(*refs, tm, n_lat, with_ctx):
    if with_ctx:
        (om_ref, od_ref, omc_ref, odc_ref, ga_ref, gb_ref, x_ref, wpm_ref, wpd_ref, wo_ref, g_ref, gate_ref,
         o_ref) = refs
    else:
        om_ref, od_ref, ga_ref, gb_ref, x_ref, wpm_ref, wpd_ref, wo_ref, g_ref, gate_ref, o_ref = refs
    i = pl.program_id(0)
    om, od, gate = om_ref[...], od_ref[...], gate_ref[0:1, :]
    if with_ctx:
        is_ctx = i * tm >= n_lat
        om = jnp.where(is_ctx, omc_ref[...], om)
        od = jnp.where(is_ctx, odc_ref[...], od)
        gate = jnp.where(is_ctx, gate_ref[1:2, :], gate)
    a = jnp.dot(om, wpm_ref[0], preferred_element_type=F32)
    b = jnp.dot(od, wpd_ref[0], preferred_element_type=F32)
    y = _sigmoid(ga_ref[...].astype(F32)) * a + _sigmoid(gb_ref[...].astype(F32)) * b
    z = jnp.dot(y.astype(BF16), wo_ref[0], preferred_element_type=F32)
    o_ref[...] = x_ref[...] + gate * _rms(z, g_ref[...])


def _post_call(o_mla, o_d, o_mla_ctx, o_d_ctx, proj, xs, wpm, wpd, wo, layer, g, gate, n_lat):
    d = xs.shape[1]
    tm = 256
    with_ctx = o_mla_ctx is not None
    n_rows = n_lat + (o_mla_ctx.shape[0] if with_ctx else 0)
    assert n_rows % tm == 0 and n_lat % tm == 0
    nb_lat = n_lat // tm
    kern = functools.partial(_post_kernel, tm=tm, n_lat=n_lat, with_ctx=with_ctx)
    const = lambda a: pl.BlockSpec((1,) + a.shape[1:], lambda i: (layer, 0, 0), pipeline_mode=pl.Buffered(1))
    lat = lambda a: pl.BlockSpec((tm, a.shape[1]), lambda i: (jnp.minimum(i, nb_lat - 1), 0))
    cxs = lambda a: pl.BlockSpec((tm, a.shape[1]), lambda i: (jnp.maximum(i - nb_lat, 0), 0))
    in_specs = [lat(o_mla), lat(o_d)]
    args = [o_mla, o_d]
    if with_ctx:
        in_specs += [cxs(o_mla_ctx), cxs(o_d_ctx)]
        args += [o_mla_ctx, o_d_ctx]
    in_specs += [pl.BlockSpec((tm, d), lambda i: (i, _OFF_GA // D_MODEL)),
                 pl.BlockSpec((tm, d), lambda i: (i, _OFF_GB // D_MODEL)),
                 pl.BlockSpec((tm, d), lambda i: (i, 0)),
                 const(wpm), const(wpd), const(wo),
                 pl.BlockSpec((1, d), lambda i: (0, 0)),
                 pl.BlockSpec((8, d), lambda i: (0, 0))]
    args += [proj, proj, xs, wpm, wpd, wo, g.reshape(1, d), gate]
    return pl.pallas_call(
        kern,
        grid=(n_rows // tm,),
        in_specs=in_specs,
        out_specs=pl.BlockSpec((tm, d), lambda i: (i, 0)),
        out_shape=jax.ShapeDtypeStruct((n_rows, d), F32),
        compiler_params=_cparams(("arbitrary",)),
        name="merge_out_proj",
    )(*args)


def _router_kernel(x_ref, g_ref, mod_ref, wr_ref, h_ref, route_ref, cnt_ref, carry_ref, *, tm, n_rows, n_lat):
    i = pl.program_id(0)

    @pl.when(i == 0)
    def _():
        carry_ref[...] = jnp.zeros_like(carry_ref)

    if n_rows > n_lat:
        is_ctx = i * tm >= n_lat
        scale = jnp.where(is_ctx, mod_ref[3:4, :], mod_ref[0:1, :])
        shift = jnp.where(is_ctx, mod_ref[4:5, :], mod_ref[1:2, :])
    else:
        scale, shift = mod_ref[0:1, :], mod_ref[1:2, :]
    h = _rms(x_ref[...], g_ref[...]) * (1.0 + scale) + shift
    h_ref[...] = h
    logits = jnp.dot(h, wr_ref[...], preferred_element_type=F32, precision=lax.Precision.HIGHEST)
    lane = lax.broadcasted_iota(jnp.int32, logits.shape, 1).astype(F32)
    lg = jnp.where(lane < N_EXPERTS, logits, -jnp.inf)
    m1 = jnp.max(lg, axis=-1, keepdims=True)
    i1 = jnp.min(jnp.where(lg == m1, lane, float(LANES)), axis=-1, keepdims=True)
    lg2 = jnp.where(lane == i1, -jnp.inf, lg)
    m2 = jnp.max(lg2, axis=-1, keepdims=True)
    i2 = jnp.min(jnp.where(lg2 == m2, lane, float(LANES)), axis=-1, keepdims=True)
    e = jnp.exp(m2 - m1)
    den = 1.0 + e

    member = jnp.where(jnp.logical_or(lane == i1, lane == i2), 1.0, 0.0)
    rr = lax.broadcasted_iota(jnp.int32, (tm, tm), 0)
    cc = lax.broadcasted_iota(jnp.int32, (tm, tm), 1)
    lower = jnp.where(rr > cc, 1.0, 0.0).astype(BF16)
    rank = jnp.dot(lower, member.astype(BF16), preferred_element_type=F32) + carry_ref[0:1, :]
    r1 = jnp.sum(jnp.where(lane == i1, rank, 0.0), axis=-1, keepdims=True)
    r2 = jnp.sum(jnp.where(lane == i2, rank, 0.0), axis=-1, keepdims=True)
    total = carry_ref[0:1, :] + jnp.sum(member, axis=0, keepdims=True)
    carry_ref[...] = jnp.broadcast_to(total, carry_ref.shape)
    cnt_ref[...] = jnp.broadcast_to(total, cnt_ref.shape)

    cols = (i1, i2, 1.0 / den, e / den, r1, r2)
    route = jnp.zeros_like(logits)
    for j, v in enumerate(cols):
        route = jnp.where(lane == float(j), v, route)
    route_ref[...] = route


def _router_call(xs, g, mod, wr_pad, n_rows, n_lat):
    d = xs.shape[1]
    tm = 256
    assert n_rows % tm == 0 and n_lat % tm == 0
    kern = functools.partial(_router_kernel, tm=tm, n_rows=n_rows, n_lat=n_lat)
    return pl.pallas_call(
        kern,
        grid=(n_rows // tm,),
        in_specs=[pl.BlockSpec((tm, d), lambda i: (i, 0)),
                  pl.BlockSpec((1, d), lambda i: (0, 0)),
                  pl.BlockSpec((8, d), lambda i: (0, 0)),
                  pl.BlockSpec((d, LANES), lambda i: (0, 0))],
        out_specs=[pl.BlockSpec((tm, d), lambda i: (i, 0)),
                   pl.BlockSpec((tm, LANES), lambda i: (i, 0)),
                   pl.BlockSpec((8, LANES), lambda i: (0, 0))],
        out_shape=[jax.ShapeDtypeStruct((n_rows, d), F32),
                   jax.ShapeDtypeStruct((n_rows, LANES), F32),
                   jax.ShapeDtypeStruct((8, LANES), F32)],
        scratch_shapes=[pltpu.VMEM((8, LANES), F32)],
        compiler_params=_cparams(("arbitrary",)),
        name="moe_router",
    )(xs, g.reshape(1, d), mod, wr_pad)


MOE_TB = 1024
MOE_SUB = 512
MOE_TF = 256
DMA_UNROLL = 8


def _dispatch_kernel(d1_ref, d2_ref, h_ref, xs_in_ref, xs_ref, sem, *, tm):
    del xs_in_ref

    def issue(t, carry):
        src = h_ref.at[pl.ds(t, 1), :]
        pltpu.make_async_copy(src, xs_ref.at[pl.ds(d1_ref[0, 0, t], 1), :], sem).start()
        pltpu.make_async_copy(src, xs_ref.at[pl.ds(d2_ref[0, 0, t], 1), :], sem).start()
        return carry

    lax.fori_loop(0, tm, issue, 0, unroll=DMA_UNROLL)

    def drain(t, carry):
        cp = pltpu.make_async_copy(h_ref.at[pl.ds(0, 1), :], xs_ref.at[pl.ds(0, 1), :], sem)
        cp.wait()
        cp.wait()
        return carry

    lax.fori_loop(0, tm, drain, 0, unroll=DMA_UNROLL)


def _dispatch_call(h, dest1, dest2, n_slots):
    n, d = h.shape
    tm = 256
    kern = functools.partial(_dispatch_kernel, tm=tm)
    idx = pl.BlockSpec((1, 1, tm), lambda i: (i, 0, 0), memory_space=pltpu.SMEM)
    anyspec = pl.BlockSpec(memory_space=pl.ANY)
    return pl.pallas_call(
        kern,
        grid=(n // tm,),
        in_specs=[idx, idx, pl.BlockSpec((tm, d), lambda i: (i, 0)), anyspec],
        out_specs=anyspec,
        out_shape=jax.ShapeDtypeStruct((n_slots, d), F32),
        scratch_shapes=[pltpu.SemaphoreType.DMA(())],
        input_output_aliases={3: 0},
        compiler_params=_cparams(("arbitrary",)),
        name="moe_dispatch",
    )(dest1.reshape(n // tm, 1, tm), dest2.reshape(n // tm, 1, tm), h, jnp.zeros((n_slots, d), F32))


def _grouped_ffn_kernel(be_ref, nv_ref, rv_ref, x_ref, w1_ref, w3_ref, w2_ref, o_ref, xb_ref):
    del be_ref, nv_ref
    b = pl.program_id(0)
    k = pl.program_id(1)
    rows = rv_ref[b]

    @pl.when(k == 0)
    def _():
        xb_ref[...] = x_ref[...].astype(BF16)
        o_ref[...] = jnp.zeros_like(o_ref)

    def swiglu_rows(n):
        w1 = w1_ref[0].astype(BF16)
        w3 = w3_ref[0].astype(BF16)
        w2 = w2_ref[0].astype(BF16)
        xb = xb_ref[0:n, :]
        u1 = jnp.dot(xb, w1, preferred_element_type=F32)
        u3 = jnp.dot(xb, w3, preferred_element_type=F32)
        u = (u1 * _sigmoid(u1)) * u3
        o_ref[0:n, :] += jnp.dot(u.astype(BF16), w2, preferred_element_type=F32)

    pl.when(rows > MOE_SUB)(lambda: swiglu_rows(MOE_TB))
    pl.when(jnp.logical_and(rows > 0, rows <= MOE_SUB))(lambda: swiglu_rows(MOE_SUB))


def _grouped_ffn_call(xs, w1, w3, w2, block_expert, n_valid, rows_valid):
    n_slots, d = xs.shape
    f = w1.shape[2]
    tb, tf = MOE_TB, MOE_TF
    n_k = f // tf
    assert n_slots % tb == 0 and f % tf == 0

    def blk(b, nv):
        return jnp.minimum(b, nv[0] - 1)

    def kk(b, k, nv):
        return jnp.where(b < nv[0], k, n_k - 1)

    grid_spec = pltpu.PrefetchScalarGridSpec(
        num_scalar_prefetch=3,
        grid=(n_slots // tb, n_k),
        in_specs=[pl.BlockSpec((tb, d), lambda b, k, be, nv, rv: (blk(b, nv), 0)),
                  pl.BlockSpec((1, d, tf), lambda b, k, be, nv, rv: (be[blk(b, nv)], 0, kk(b, k, nv))),
                  pl.BlockSpec((1, d, tf), lambda b, k, be, nv, rv: (be[blk(b, nv)], 0, kk(b, k, nv))),
                  pl.BlockSpec((1, tf, d), lambda b, k, be, nv, rv: (be[blk(b, nv)], kk(b, k, nv), 0))],
        out_specs=pl.BlockSpec((tb, d), lambda b, k, be, nv, rv: (b, 0)),
        scratch_shapes=[pltpu.VMEM((tb, d), BF16)])
    return pl.pallas_call(
        _grouped_ffn_kernel,
        grid_spec=grid_spec,
        out_shape=jax.ShapeDtypeStruct((n_slots, d), F32),
        compiler_params=_cparams(("arbitrary", "arbitrary")),
        name="moe_grouped_ffn",
    )(block_expert, n_valid, rows_valid, xs, w1, w3, w2)


def _combine_kernel(d1_ref, d2_ref, route_ref, x_ref, g3_ref, mod_ref, y_ref, o_ref, ya_ref, yb_ref, sem,
                    *, tm, n_rows, n_lat):
    i = pl.program_id(0)

    def issue(t, carry):
        pltpu.make_async_copy(y_ref.at[pl.ds(d1_ref[0, 0, t], 1), :], ya_ref.at[pl.ds(t, 1), :], sem).start()
        pltpu.make_async_copy(y_ref.at[pl.ds(d2_ref[0, 0, t], 1), :], yb_ref.at[pl.ds(t, 1), :], sem).start()
        return carry

    lax.fori_loop(0, tm, issue, 0, unroll=DMA_UNROLL)

    def drain(t, carry):
        cp = pltpu.make_async_copy(y_ref.at[pl.ds(0, 1), :], ya_ref.at[pl.ds(0, 1), :], sem)
        cp.wait()
        cp.wait()
        return carry

    lax.fori_loop(0, tm, drain, 0, unroll=DMA_UNROLL)

    r = route_ref[...]
    f = r[:, 2:3] * ya_ref[...] + r[:, 3:4] * yb_ref[...]
    if n_rows > n_lat:
        gate = jnp.where(i * tm >= n_lat, mod_ref[5:6, :], mod_ref[2:3, :])
    else:
        gate = mod_ref[2:3, :]
    o_ref[...] = x_ref[...] + gate * _rms(f, g3_ref[...])


def _combine_call(y, dest1, dest2, route, xs, g3, mod, n_rows, n_lat):
    d = xs.shape[1]
    tm = 256
    kern = functools.partial(_combine_kernel, tm=tm, n_rows=n_rows, n_lat=n_lat)
    idx = pl.BlockSpec((1, 1, tm), lambda i: (i, 0, 0), memory_space=pltpu.SMEM)
    return pl.pallas_call(
        kern,
        grid=(n_rows // tm,),
        in_specs=[idx, idx,
                  pl.BlockSpec((tm, LANES), lambda i: (i, 0)),
                  pl.BlockSpec((tm, d), lambda i: (i, 0)),
                  pl.BlockSpec((1, d), lambda i: (0, 0)),
                  pl.BlockSpec((8, d), lambda i: (0, 0)),
                  pl.BlockSpec(memory_space=pl.ANY)],
        out_specs=pl.BlockSpec((tm, d), lambda i: (i, 0)),
        out_shape=jax.ShapeDtypeStruct((n_rows, d), F32),
        scratch_shapes=[pltpu.VMEM((tm, d), F32), pltpu.VMEM((tm, d), F32), pltpu.SemaphoreType.DMA(())],
        compiler_params=_cparams(("arbitrary",)),
        name="moe_combine",
    )(dest1.reshape(n_rows // tm, 1, tm), dest2.reshape(n_rows // tm, 1, tm), route, xs, g3.reshape(1, d), mod, y)


def _moe_call(xs, g2, mod, wr_pad, w1, w3, w2, li, g3, n_rows, n_lat):
    h, route, cnt = _router_call(xs, g2, mod, wr_pad, n_rows, n_lat)
    tb = MOE_TB
    n_slots = -(-(2 * n_rows + N_EXPERTS * (tb - 1)) // tb) * tb
    counts = cnt[0, :N_EXPERTS].astype(jnp.int32)
    padded = (counts + tb - 1) // tb * tb
    ends = jnp.cumsum(padded)
    offs = ends - padded
    e1, e2 = route[:, 0].astype(jnp.int32), route[:, 1].astype(jnp.int32)
    dest1 = offs[e1] + route[:, 4].astype(jnp.int32)
    dest2 = offs[e2] + route[:, 5].astype(jnp.int32)
    blk_start = jnp.arange(n_slots // tb, dtype=jnp.int32) * tb
    block_expert = jnp.minimum(jnp.sum(blk_start[:, None] >= ends[None, :], axis=1), N_EXPERTS - 1)
    n_valid = (ends[-1:] // tb).astype(jnp.int32)
    rows_valid = jnp.clip((offs + counts)[block_expert] - blk_start, 0, tb).astype(jnp.int32)
    rows_valid = jnp.where(blk_start < ends[-1], rows_valid, 0)
    xg = _dispatch_call(h, dest1, dest2, n_slots)
    slab = (block_expert + li * N_EXPERTS).astype(jnp.int32)
    y = _grouped_ffn_call(xg, w1, w3, w2, slab, n_valid, rows_valid)
    return _combine_call(y, dest1, dest2, route, xs, g3, mod, n_rows, n_lat)


def _ffn_kernel(x_ref, g2_ref, mod_ref, w1_ref, w3_ref, w2_ref, g3_ref, o_ref, h_ref, acc_ref,
                *, tm, n_rows, n_lat, n_k):
    i = pl.program_id(0)
    k = pl.program_id(1)

    @pl.when(k == 0)
    def _():
        def fill(r0, r1, is_ctx):
            r = 3 if is_ctx else 0
            h = _rms(x_ref[r0:r1, :], g2_ref[...]) * (1.0 + mod_ref[r:r + 1, :]) + mod_ref[r + 1:r + 2, :]
            h_ref[r0:r1, :] = h.astype(BF16)
        _row_groups(i, tm, n_rows, n_lat, fill)
        acc_ref[...] = jnp.zeros_like(acc_ref)

    h = h_ref[...]
    u1 = jnp.dot(h, w1_ref[0], preferred_element_type=F32)
    u3 = jnp.dot(h, w3_ref[0], preferred_element_type=F32)
    u = (u1 * _sigmoid(u1)) * u3
    acc_ref[...] += jnp.dot(u.astype(BF16), w2_ref[0], preferred_element_type=F32)

    @pl.when(k == n_k - 1)
    def _():
        def fin(r0, r1, is_ctx):
            r = 5 if is_ctx else 2
            fn = _rms(acc_ref[r0:r1, :], g3_ref[...])
            o_ref[r0:r1, :] = x_ref[r0:r1, :] + mod_ref[r:r + 1, :] * fn
        _row_groups(i, tm, n_rows, n_lat, fin)


def _ffn_call(xs, g2, mod, w1, w3, w2, li, g3, n_rows, n_lat):
    d = xs.shape[1]
    f = w1.shape[2]
    tm = _pick(n_rows, (768, 512, 256))
    tf = _pick(f, (512, 256, 128))
    n_k = f // tf
    kern = functools.partial(_ffn_kernel, tm=tm, n_rows=n_rows, n_lat=n_lat, n_k=n_k)
    return pl.pallas_call(
        kern,
        grid=(n_rows // tm, n_k),
        in_specs=[pl.BlockSpec((tm, d), lambda i, k: (i, 0)),
                  pl.BlockSpec((1, d), lambda i, k: (0, 0)),
                  pl.BlockSpec((8, d), lambda i, k: (0, 0)),
                  pl.BlockSpec((1, d, tf), lambda i, k: (li, 0, k)),
                  pl.BlockSpec((1, d, tf), lambda i, k: (li, 0, k)),
                  pl.BlockSpec((1, tf, d), lambda i, k: (li, k, 0)),
                  pl.BlockSpec((1, d), lambda i, k: (0, 0))],
        out_specs=pl.BlockSpec((tm, d), lambda i, k: (i, 0)),
        out_shape=jax.ShapeDtypeStruct((n_rows, d), F32),
        scratch_shapes=[pltpu.VMEM((tm, d), BF16), pltpu.VMEM((tm, d), F32)],
        compiler_params=_cparams(("arbitrary", "arbitrary")),
        name="dense_ffn",
    )(xs, g2.reshape(1, d), mod, w1, w3, w2, g3.reshape(1, d))


def _prep_w_in(w_in):
    o_kr, o_dq = MLA_Q_RANK + MLA_KV_RANK, MLA_Q_RANK + MLA_KV_RANK + MLA_ROPE
    pad = jnp.zeros(w_in.shape[:-1] + (D_IN_PAD - w_in.shape[-1],), w_in.dtype)
    return jnp.concatenate([w_in[..., :o_kr], w_in[..., o_dq:], w_in[..., o_kr:o_dq], pad], axis=-1).astype(BF16)


def _prep_w_uq(w_uq):
    lead = w_uq.shape[:-1]
    w = w_uq.reshape(lead + (MLA_HEADS, MLA_NOPE + MLA_ROPE))
    w = jnp.pad(w, [(0, 0)] * len(lead) + [(0, 0), (0, MLA_QK_PAD - MLA_NOPE - MLA_ROPE)])
    return w.reshape(lead + (MLA_HEADS * MLA_QK_PAD,)).astype(BF16)


def _prep_w_ukv(w_ukv):
    lead = w_ukv.shape[:-1]
    w = w_ukv.reshape(lead + (MLA_HEADS, MLA_NOPE + MLA_V))
    wk = w[..., :MLA_NOPE].reshape(lead + (MLA_HEADS * MLA_NOPE,))
    wv = w[..., MLA_NOPE:].reshape(lead + (MLA_HEADS * MLA_V,))
    return wk.astype(BF16), wv.astype(BF16)


def _rope_tables(n_lat, n_ctx):
    t = jnp.arange(n_lat, dtype=jnp.int32)
    row = (t // GRID_W).astype(F32)[:, None]
    col = (t % GRID_W).astype(F32)[:, None]

    def table(half):
        freqs = ROPE_BASE ** (-jnp.arange(half, dtype=F32) / half)
        ar, ac = row * freqs, col * freqs
        z = jnp.zeros((n_lat, half), F32)
        pad = jnp.zeros((n_lat, LANES - 4 * half), F32)
        cos = jnp.concatenate([jnp.cos(ar), jnp.cos(ar), jnp.cos(ac), jnp.cos(ac), 1.0 + pad], axis=-1)
        sin_lo = jnp.concatenate([-jnp.sin(ar), z, -jnp.sin(ac), z, pad], axis=-1)
        sin_hi = jnp.concatenate([z, jnp.sin(ar), z, jnp.sin(ac), pad], axis=-1)
        ident = jnp.ones((n_ctx, LANES), F32), jnp.zeros((n_ctx, LANES), F32), jnp.zeros((n_ctx, LANES), F32)
        return tuple(jnp.concatenate([a, b], axis=0) for a, b in zip((cos, sin_lo, sin_hi), ident))

    return table(DIFF_QK // 4) + table(MLA_ROPE // 4)


def kernel(x, c, ctx, c_ctx, w_mod, b_mod, g_norm, w_in, w_uq, g_qn, w_ukv, g_kvn, lam, g_sub, w_po_mla,
           w_po_diff, w_out, w1_dense, w3_dense, w2_dense, w_router, w1_moe, w3_moe, w2_moe):
    b, s, d = x.shape
    n_ctx = ctx.shape[1]
    depth = w_mod.shape[0]
    assert b == 1 and d == D_MODEL and s % GRID_W == 0
    n_tok = s + n_ctx

    xs = jnp.concatenate([x[0], ctx[0]], axis=0)
    c8 = jnp.concatenate([c, c_ctx[None, :], jnp.zeros((6, d), F32)], axis=0)
    mods = _mod_call(c8, w_mod, b_mod)

    w_in_p = _prep_w_in(w_in)
    w_uq_p = _prep_w_uq(w_uq)
    w_k_p, w_v_p = _prep_w_ukv(w_ukv)
    w_pm, w_pd, w_o = w_po_mla.astype(BF16), w_po_diff.astype(BF16), w_out.astype(BF16)
    w1d, w3d, w2d = w1_dense.astype(BF16), w3_dense.astype(BF16), w2_dense.astype(BF16)
    wr_pad = jnp.pad(w_router, ((0, 0), (0, 0), (0, LANES - N_EXPERTS)))
    flat = lambda w: w.reshape((-1,) + w.shape[2:])
    w1m, w3m, w2m = flat(w1_moe), flat(w3_moe), flat(w2_moe)
    tabs = _rope_tables(s, n_ctx)
    zrow = jnp.zeros((d,), F32)

    for layer in range(depth):
        need_ctx = layer < depth - 1
        lat, cx = mods[layer, 0], mods[layer, 1]
        seg = lambda v, k: v[k * d:(k + 1) * d]
        gn = g_norm[layer]

        mod1 = jnp.stack([seg(lat, 1), seg(lat, 0), seg(cx, 1), seg(cx, 0), zrow, zrow, zrow, zrow])
        proj = _norm_matmul_call(xs, gn[0], mod1, w_in_p, layer, s)
        qm, km, vm, qd, kd = _prep_call(proj, g_qn[layer].reshape(1, -1), g_kvn[layer].reshape(1, -1),
                                        w_uq_p, w_k_p, w_v_p, layer, tabs)
        lam_init = 0.8 - 0.6 * math.exp(-0.3 * layer)
        o_mla = _mla_attn_call(qm, km, vm, 0, s, 0, n_tok)
        o_d = _diff_attn_call(qd, kd, proj, lam[layer], g_sub[layer], lam_init, 0, s, 0, n_tok)
        o_mla_c = o_d_c = None
        if need_ctx:
            o_mla_c = _mla_attn_call(qm, km, vm, s, n_ctx, s, n_ctx)
            o_d_c = _diff_attn_call(qd, kd, proj, lam[layer], g_sub[layer], lam_init, s, n_ctx, s, n_ctx)
        n_rows = n_tok if need_ctx else s
        gate1 = jnp.stack([seg(lat, 2), seg(cx, 2), zrow, zrow, zrow, zrow, zrow, zrow])
        xs = _post_call(o_mla, o_d, o_mla_c, o_d_c, proj, xs, w_pm, w_pd, w_o, layer, gn[1], gate1, s)

        mod2 = jnp.stack([seg(lat, 4), seg(lat, 3), seg(lat, 5), seg(cx, 4), seg(cx, 3), seg(cx, 5), zrow, zrow])
        i = layer // 2
        if layer % 2 == 0:
            xs = _ffn_call(xs, gn[2], mod2, w1d, w3d, w2d, i, gn[3], n_rows, s)
        else:
            xs = _moe_call(xs, gn[2], mod2, wr_pad[i], w1m, w3m, w2m, i, gn[3], n_rows, s)
    return xs[:s][None]
```

```python
import functools
import math

import jax
import jax.numpy as jnp
from jax import lax
from jax.experimental import pallas as pl
from jax.experimental.pallas import tpu as pltpu

F32 = jnp.float32
BF16 = jnp.bfloat16

D_MODEL = 2048
GRID_W = 64
MLA_HEADS = 8
MLA_Q_RANK = 512
MLA_KV_RANK = 512
MLA_NOPE = 128
MLA_ROPE = 64
MLA_V = 128
MLA_QK_PAD = 256
DIFF_HEADS = 4
DIFF_QK = 128
DIFF_V = 256
N_EXPERTS = 8
ROPE_BASE = 10000.0
NORM_EPS = 1e-6
LANES = 128
LOG2E = math.log2(math.e)

_OFF_CQ, _OFF_CKV, _OFF_DQ, _OFF_DK, _OFF_DV, _OFF_GA, _OFF_GB, _OFF_KR = (
    0, 512, 1024, 2048, 3072, 4096, 6144, 8192)
D_IN_PAD = 8448

VMEM_LIMIT = 56 * 1024 * 1024


def _pick(n, cands):
    for c in cands:
        if n % c == 0:
            return c
    raise ValueError(f"no tile for {n} in {cands}")


def _cparams(sem):
    return pltpu.CompilerParams(dimension_semantics=sem, vmem_limit_bytes=VMEM_LIMIT)


def _rms(xf, g):
    ms = jnp.mean(xf * xf, axis=-1, keepdims=True)
    return xf * lax.rsqrt(ms + NORM_EPS) * g


def _sigmoid(v):
    return 1.0 / (1.0 + jnp.exp(-v))


def _row_groups(i, tm, n_rows, n_lat, fn):
    if n_rows <= n_lat:
        fn(0, tm, False)
        return
    nb = n_rows // tm
    b, off = divmod(n_lat, tm)
    if b > 0:
        pl.when(i < b)(lambda: fn(0, tm, False))
    if off:
        def _split():
            fn(0, off, False)
            fn(off, tm, True)
        pl.when(i == b)(_split)
        if nb > b + 1:
            pl.when(i > b)(lambda: fn(0, tm, True))
    else:
        pl.when(i >= b)(lambda: fn(0, tm, True))


def _mod_kernel(c_ref, w_ref, b_ref, o_ref):
    cv = c_ref[...]
    s = cv * _sigmoid(cv)
    o_ref[0] = jnp.dot(s, w_ref[0], preferred_element_type=F32,
                       precision=lax.Precision.HIGHEST) + b_ref[0]


def _mod_call(c8, w_mod, b_mod):
    depth, d, n6 = w_mod.shape
    tn = _pick(n6, (1536, 1024, 512, 256, 128))
    return pl.pallas_call(
        _mod_kernel,
        grid=(depth, n6 // tn),
        in_specs=[pl.BlockSpec((8, d), lambda l, j: (0, 0)),
                  pl.BlockSpec((1, d, tn), lambda l, j: (l, 0, j)),
                  pl.BlockSpec((1, 1, tn), lambda l, j: (l, 0, j))],
        out_specs=pl.BlockSpec((1, 8, tn), lambda l, j: (l, 0, j)),
        out_shape=jax.ShapeDtypeStruct((depth, 8, n6), F32),
        compiler_params=_cparams(("arbitrary", "arbitrary")),
        name="adaln_mod",
    )(c8, w_mod, b_mod.reshape(depth, 1, n6))


def _norm_matmul_kernel(x_ref, g_ref, mod_ref, w_ref, o_ref, h_ref, *, tm, n_rows, n_lat):
    i = pl.program_id(0)
    j = pl.program_id(1)

    @pl.when(j == 0)
    def _():
        def fill(r0, r1, is_ctx):
            r = 2 if is_ctx else 0
            h = _rms(x_ref[r0:r1, :], g_ref[...]) * (1.0 + mod_ref[r:r + 1, :]) + mod_ref[r + 1:r + 2, :]
            h_ref[r0:r1, :] = h.astype(BF16)
        _row_groups(i, tm, n_rows, n_lat, fill)

    o_ref[...] = jnp.dot(h_ref[...], w_ref[0], preferred_element_type=F32).astype(o_ref.dtype)


def _norm_matmul_call(xs, g, mod, w, layer, n_lat):
    n_rows, d = xs.shape
    n_out = w.shape[2]
    tm = _pick(n_rows, (768, 512, 256))
    tn = _pick(n_out, (1408, 1024, 768, 512, 256))
    kern = functools.partial(_norm_matmul_kernel, tm=tm, n_rows=n_rows, n_lat=n_lat)
    return pl.pallas_call(
        kern,
        grid=(n_rows // tm, n_out // tn),
        in_specs=[pl.BlockSpec((tm, d), lambda i, j: (i, 0)),
                  pl.BlockSpec((1, d), lambda i, j: (0, 0)),
                  pl.BlockSpec((8, d), lambda i, j: (0, 0)),
                  pl.BlockSpec((1, d, tn), lambda i, j: (layer, 0, j))],
        out_specs=pl.BlockSpec((tm, tn), lambda i, j: (i, j)),
        out_shape=jax.ShapeDtypeStruct((n_rows, n_out), BF16),
        scratch_shapes=[pltpu.VMEM((tm, d), BF16)],
        compiler_params=_cparams(("arbitrary", "arbitrary")),
        name="norm_in_proj",
    )(xs, g.reshape(1, d), mod, w)


def _rope(xf, tab, half):
    cos, sin_lo, sin_hi = tab
    return xf * cos + pltpu.roll(xf, LANES - half, 1) * sin_lo + pltpu.roll(xf, half, 1) * sin_hi


def _prep_kernel(cq_ref, ckv_ref, dq_ref, dk_ref, kr_ref, gq_ref, gkv_ref, wuq_ref, wk_ref, wv_ref,
                 cosd_ref, sld_ref, shd_ref, cosm_ref, slm_ref, shm_ref,
                 qm_ref, km_ref, vm_ref, qd_ref, kd_ref, *, s_mla, s_diff):
    tabm = (cosm_ref[...], slm_ref[...], shm_ref[...])
    tabd = (cosd_ref[...], sld_ref[...], shd_ref[...])
    half_m, half_d = MLA_ROPE // 4, DIFF_QK // 4

    cqn = _rms(cq_ref[...].astype(F32), gq_ref[...]).astype(BF16)
    q = jnp.dot(cqn, wuq_ref[0], preferred_element_type=F32)
    for h in range(MLA_HEADS):
        base = h * MLA_QK_PAD
        qm_ref[h, :, 0:LANES] = (q[:, base:base + LANES] * s_mla).astype(BF16)
        qr = _rope(q[:, base + LANES:base + 2 * LANES], tabm, half_m)
        qm_ref[h, :, LANES:2 * LANES] = (qr * s_mla).astype(BF16)

    ckvn = _rms(ckv_ref[...].astype(F32), gkv_ref[...]).astype(BF16)
    kn = jnp.dot(ckvn, wk_ref[0], preferred_element_type=F32)
    vv = jnp.dot(ckvn, wv_ref[0], preferred_element_type=F32)
    kr = _rope(kr_ref[...].astype(F32), tabm, half_m).astype(BF16)
    for h in range(MLA_HEADS):
        km_ref[h, :, 0:LANES] = kn[:, h * LANES:(h + 1) * LANES].astype(BF16)
        km_ref[h, :, LANES:2 * LANES] = kr
        vm_ref[h] = vv[:, h * LANES:(h + 1) * LANES].astype(BF16)

    for hc in range(2 * DIFF_HEADS):
        xq = dq_ref[:, hc * LANES:(hc + 1) * LANES].astype(F32)
        qd_ref[hc] = (_rope(xq, tabd, half_d) * s_diff).astype(BF16)
        xk = dk_ref[:, hc * LANES:(hc + 1) * LANES].astype(F32)
        kd_ref[hc] = _rope(xk, tabd, half_d).astype(BF16)


def _prep_call(proj, gq, gkv, wuq, wk, wv, layer, tabs):
    n = proj.shape[0]
    tm = _pick(n, (384, 256, 128))
    nh, nd = MLA_HEADS, 2 * DIFF_HEADS
    kern = functools.partial(_prep_kernel, s_mla=float((MLA_NOPE + MLA_ROPE) ** -0.5 * LOG2E),
                             s_diff=float(DIFF_QK ** -0.5 * LOG2E))
    row = lambda w, c: pl.BlockSpec((tm, w), lambda i, c=c: (i, c))
    const = lambda a: pl.BlockSpec(a.shape, lambda i: (0,) * a.ndim)
    wspec = lambda a: pl.BlockSpec((1,) + a.shape[1:], lambda i: (layer, 0, 0))
    tab = pl.BlockSpec((tm, LANES), lambda i: (i, 0))
    return pl.pallas_call(
        kern,
        grid=(n // tm,),
        in_specs=[row(512, _OFF_CQ // 512), row(512, _OFF_CKV // 512), row(1024, _OFF_DQ // 1024),
                  row(1024, _OFF_DK // 1024), row(LANES, _OFF_KR // LANES),
                  const(gq), const(gkv), wspec(wuq), wspec(wk), wspec(wv)] + [tab] * 6,
        out_specs=[pl.BlockSpec((nh, tm, MLA_QK_PAD), lambda i: (0, i, 0)),
                   pl.BlockSpec((nh, tm, MLA_QK_PAD), lambda i: (0, i, 0)),
                   pl.BlockSpec((nh, tm, MLA_V), lambda i: (0, i, 0)),
                   pl.BlockSpec((nd, tm, DIFF_QK), lambda i: (0, i, 0)),
                   pl.BlockSpec((nd, tm, DIFF_QK), lambda i: (0, i, 0))],
        out_shape=[jax.ShapeDtypeStruct((nh, n, MLA_QK_PAD), BF16),
                   jax.ShapeDtypeStruct((nh, n, MLA_QK_PAD), BF16),
                   jax.ShapeDtypeStruct((nh, n, MLA_V), BF16),
                   jax.ShapeDtypeStruct((nd, n, DIFF_QK), BF16),
                   jax.ShapeDtypeStruct((nd, n, DIFF_QK), BF16)],
        compiler_params=_cparams(("arbitrary",)),
        name="qkv_prep",
    )(proj, proj, proj, proj, proj, gq, gkv, wuq, wk, wv, *tabs)


def _softmax_pv(q, k_at, v_at, tk, n_chunks, dv):
    tq = q.shape[0]
    m = jnp.full((tq, 1), -jnp.inf, F32)
    l = jnp.zeros((tq, 1), F32)
    acc = jnp.zeros((tq, dv), F32)
    for c in range(n_chunks):
        off = c * tk
        s = lax.dot_general(q, k_at(off), (((1,), (1,)), ((), ())), preferred_element_type=F32)
        m_new = jnp.maximum(m, jnp.max(s, axis=-1, keepdims=True))
        alpha = jnp.exp2(m - m_new)
        p = jnp.exp2(s - m_new)
        l = alpha * l + jnp.sum(p, axis=-1, keepdims=True)
        acc = alpha * acc + jnp.dot(p.astype(BF16), v_at(off), preferred_element_type=F32)
        m = m_new
    return acc / l


def _mla_attn_kernel(q_ref, k_ref, v_ref, o_ref, *, tk, n_chunks, n_sub):
    ts = q_ref.shape[1] // n_sub
    qs = [q_ref[0, i * ts:(i + 1) * ts, :] for i in range(n_sub)]
    m = [jnp.full((ts, 1), -jnp.inf, F32) for _ in range(n_sub)]
    l = [jnp.zeros((ts, 1), F32) for _ in range(n_sub)]
    acc = [jnp.zeros((ts, MLA_V), F32) for _ in range(n_sub)]
    for c in range(n_chunks):
        k = k_ref[0, c * tk:(c + 1) * tk, :]
        v = v_ref[0, c * tk:(c + 1) * tk, :]
        for i in range(n_sub):
            s = lax.dot_general(qs[i], k, (((1,), (1,)), ((), ())), preferred_element_type=F32)
            m_new = jnp.maximum(m[i], jnp.max(s, axis=-1, keepdims=True))
            alpha = jnp.exp2(m[i] - m_new)
            p = jnp.exp2(s - m_new)
            l[i] = alpha * l[i] + jnp.sum(p, axis=-1, keepdims=True)
            acc[i] = alpha * acc[i] + jnp.dot(p.astype(BF16), v, preferred_element_type=F32)
            m[i] = m_new
    for i in range(n_sub):
        o_ref[i * ts:(i + 1) * ts, :] = (acc[i] / l[i]).astype(o_ref.dtype)


def _mla_attn_call(qm, km, vm, q_row0, n_q, kv_row0, n_kv):
    tq = _pick(n_q, (1024, 512, 256, 128))
    tk = _pick(n_kv, (2816, 768, 512, 256, 128))
    assert q_row0 % tq == 0 and kv_row0 % n_kv == 0
    qb, kb = q_row0 // tq, kv_row0 // n_kv
    kern = functools.partial(_mla_attn_kernel, tk=tk, n_chunks=n_kv // tk, n_sub=max(1, tq // 512))
    return pl.pallas_call(
        kern,
        grid=(MLA_HEADS, n_q // tq),
        in_specs=[pl.BlockSpec((1, tq, MLA_QK_PAD), lambda h, i: (h, i + qb, 0)),
                  pl.BlockSpec((1, n_kv, MLA_QK_PAD), lambda h, i: (h, kb, 0)),
                  pl.BlockSpec((1, n_kv, MLA_V), lambda h, i: (h, kb, 0))],
        out_specs=pl.BlockSpec((tq, MLA_V), lambda h, i: (i, h)),
        out_shape=jax.ShapeDtypeStruct((n_q, MLA_HEADS * MLA_V), BF16),
        compiler_params=_cparams(("arbitrary", "arbitrary")),
        name="mla_attention",
    )(qm, km, vm)


def _diff_attn_kernel(q_ref, k_ref, v_ref, lam_ref, gs_ref, o_ref, *, tk, n_chunks):
    lam_init = lam_ref[4:5, 0:1]
    outs = []
    for comp in range(2):
        outs.append(_softmax_pv(q_ref[comp],
                                lambda off, comp=comp: k_ref[comp, pl.ds(off, tk), :],
                                lambda off: v_ref[pl.ds(off, tk), :],
                                tk, n_chunks, DIFF_V))
    lf = lam_ref[...]
    lam_full = (jnp.exp(jnp.sum(lf[0:1] * lf[1:2], axis=-1, keepdims=True))
                - jnp.exp(jnp.sum(lf[2:3] * lf[3:4], axis=-1, keepdims=True)) + lam_init)
    d = outs[0] - lam_full * outs[1]
    o_ref[...] = (_rms(d, gs_ref[...]) * (1.0 - lam_init)).astype(o_ref.dtype)


def _diff_attn_call(qd, kd, proj, lam, gsub, lam_init, q_row0, n_q, kv_row0, n_kv):
    tq = _pick(n_q, (512, 256, 128))
    tk = _pick(n_kv, (2816, 768, 512, 256, 128))
    assert q_row0 % tq == 0 and kv_row0 % n_kv == 0
    qb, kb = q_row0 // tq, kv_row0 // n_kv
    vcol = _OFF_DV // DIFF_V
    kern = functools.partial(_diff_attn_kernel, tk=tk, n_chunks=n_kv // tk)
    in_specs = [pl.BlockSpec((2, tq, DIFF_QK), lambda h, i: (h, i + qb, 0)),
                pl.BlockSpec((2, n_kv, DIFF_QK), lambda h, i: (h, kb, 0)),
                pl.BlockSpec((n_kv, DIFF_V), lambda h, i: (kb, vcol + h)),
                pl.BlockSpec((8, DIFF_QK), lambda h, i: (0, 0)),
                pl.BlockSpec((1, DIFF_V), lambda h, i: (0, 0))]
    lam8 = jnp.concatenate([lam, jnp.full((4, DIFF_QK), lam_init, F32)], axis=0)
    return pl.pallas_call(
        kern,
        grid=(DIFF_HEADS, n_q // tq),
        in_specs=in_specs,
        out_specs=pl.BlockSpec((tq, DIFF_V), lambda h, i: (i, h)),
        out_shape=jax.ShapeDtypeStruct((n_q, DIFF_HEADS * DIFF_V), BF16),
        compiler_params=_cparams(("arbitrary", "arbitrary")),
        name="diff_attention",
    )(qd, kd, proj, lam8, gsub.reshape(1, DIFF_V))


def _post_kernel(*refs, tm, n_lat, with_ctx):
    if with_ctx:
        (om_ref, od_ref, omc_ref, odc_ref, ga_ref, gb_ref, x_ref, wpm_ref, wpd_ref, wo_ref, g_ref, gate_ref,
         o_ref) = refs
    else:
        om_ref, od_ref, ga_ref, gb_ref, x_ref, wpm_ref, wpd_ref, wo_ref, g_ref, gate_ref, o_ref = refs
    i = pl.program_id(0)
    om, od, gate = om_ref[...], od_ref[...], gate_ref[0:1, :]
    if with_ctx:
        is_ctx = i * tm >= n_lat
        om = jnp.where(is_ctx, omc_ref[...], om)
        od = jnp.where(is_ctx, odc_ref[...], od)
        gate = jnp.where(is_ctx, gate_ref[1:2, :], gate)
    a = jnp.dot(om, wpm_ref[0], preferred_element_type=F32)
    b = jnp.dot(od, wpd_ref[0], preferred_element_type=F32)
    y = _sigmoid(ga_ref[...].astype(F32)) * a + _sigmoid(gb_ref[...].astype(F32)) * b
    z = jnp.dot(y.astype(BF16), wo_ref[0], preferred_element_type=F32)
    o_ref[...] = x_ref[...] + gate * _rms(z, g_ref[...])


def _post_call(o_mla, o_d, o_mla_ctx, o_d_ctx, proj, xs, wpm, wpd, wo, layer, g, gate, n_lat):
    d = xs.shape[1]
    tm = 256
    with_ctx = o_mla_ctx is not None
    n_rows = n_lat + (o_mla_ctx.shape[0] if with_ctx else 0)
    assert n_rows % tm == 0 and n_lat % tm == 0
    nb_lat = n_lat // tm
    kern = functools.partial(_post_kernel, tm=tm, n_lat=n_lat, with_ctx=with_ctx)
    const = lambda a: pl.BlockSpec((1,) + a.shape[1:], lambda i: (layer, 0, 0), pipeline_mode=pl.Buffered(1))
    lat = lambda a: pl.BlockSpec((tm, a.shape[1]), lambda i: (jnp.minimum(i, nb_lat - 1), 0))
    cxs = lambda a: pl.BlockSpec((tm, a.shape[1]), lambda i: (jnp.maximum(i - nb_lat, 0), 0))
    in_specs = [lat(o_mla), lat(o_d)]
    args = [o_mla, o_d]
    if with_ctx:
        in_specs += [cxs(o_mla_ctx), cxs(o_d_ctx)]
        args += [o_mla_ctx, o_d_ctx]
    in_specs += [pl.BlockSpec((tm, d), lambda i: (i, _OFF_GA // D_MODEL)),
                 pl.BlockSpec((tm, d), lambda i: (i, _OFF_GB // D_MODEL)),
                 pl.BlockSpec((tm, d), lambda i: (i, 0)),
                 const(wpm), const(wpd), const(wo),
                 pl.BlockSpec((1, d), lambda i: (0, 0)),
                 pl.BlockSpec((8, d), lambda i: (0, 0))]
    args += [proj, proj, xs, wpm, wpd, wo, g.reshape(1, d), gate]
    return pl.pallas_call(
        kern,
        grid=(n_rows // tm,),
        in_specs=in_specs,
        out_specs=pl.BlockSpec((tm, d), lambda i: (i, 0)),
        out_shape=jax.ShapeDtypeStruct((n_rows, d), F32),
        compiler_params=_cparams(("arbitrary",)),
        name="merge_out_proj",
    )(*args)


def _router_kernel(x_ref, g_ref, mod_ref, wr_ref, h_ref, route_ref, cnt_ref, carry_ref, *, tm, n_rows, n_lat):
    i = pl.program_id(0)

    @pl.when(i == 0)
    def _():
        carry_ref[...] = jnp.zeros_like(carry_ref)

    if n_rows > n_lat:
        is_ctx = i * tm >= n_lat
        scale = jnp.where(is_ctx, mod_ref[3:4, :], mod_ref[0:1, :])
        shift = jnp.where(is_ctx, mod_ref[4:5, :], mod_ref[1:2, :])
    else:
        scale, shift = mod_ref[0:1, :], mod_ref[1:2, :]
    h = _rms(x_ref[...], g_ref[...]) * (1.0 + scale) + shift
    h_ref[...] = h
    logits = jnp.dot(h, wr_ref[...], preferred_element_type=F32, precision=lax.Precision.HIGHEST)
    lane = lax.broadcasted_iota(jnp.int32, logits.shape, 1).astype(F32)
    lg = jnp.where(lane < N_EXPERTS, logits, -jnp.inf)
    m1 = jnp.max(lg, axis=-1, keepdims=True)
    i1 = jnp.min(jnp.where(lg == m1, lane, float(LANES)), axis=-1, keepdims=True)
    lg2 = jnp.where(lane == i1, -jnp.inf, lg)
    m2 = jnp.max(lg2, axis=-1, keepdims=True)
    i2 = jnp.min(jnp.where(lg2 == m2, lane, float(LANES)), axis=-1, keepdims=True)
    e = jnp.exp(m2 - m1)
    den = 1.0 + e

    member = jnp.where(jnp.logical_or(lane == i1, lane == i2), 1.0, 0.0)
    rr = lax.broadcasted_iota(jnp.int32, (tm, tm), 0)
    cc = lax.broadcasted_iota(jnp.int32, (tm, tm), 1)
    lower = jnp.where(rr > cc, 1.0, 0.0).astype(BF16)
    rank = jnp.dot(lower, member.astype(BF16), preferred_element_type=F32) + carry_ref[0:1, :]
    r1 = jnp.sum(jnp.where(lane == i1, rank, 0.0), axis=-1, keepdims=True)
    r2 = jnp.sum(jnp.where(lane == i2, rank, 0.0), axis=-1, keepdims=True)
    total = carry_ref[0:1, :] + jnp.sum(member, axis=0, keepdims=True)
    carry_ref[...] = jnp.broadcast_to(total, carry_ref.shape)
    cnt_ref[...] = jnp.broadcast_to(total, cnt_ref.shape)

    cols = (i1, i2, 1.0 / den, e / den, r1, r2)
    route = jnp.zeros_like(logits)
    for j, v in enumerate(cols):
        route = jnp.where(lane == float(j), v, route)
    route_ref[...] = route


def _router_call(xs, g, mod, wr_pad, n_rows, n_lat):
    d = xs.shape[1]
    tm = 256
    assert n_rows % tm == 0 and n_lat % tm == 0
    kern = functools.partial(_router_kernel, tm=tm, n_rows=n_rows, n_lat=n_lat)
    return pl.pallas_call(
        kern,
        grid=(n_rows // tm,),
        in_specs=[pl.BlockSpec((tm, d), lambda i: (i, 0)),
                  pl.BlockSpec((1, d), lambda i: (0, 0)),
                  pl.BlockSpec((8, d), lambda i: (0, 0)),
                  pl.BlockSpec((d, LANES), lambda i: (0, 0))],
        out_specs=[pl.BlockSpec((tm, d), lambda i: (i, 0)),
                   pl.BlockSpec((tm, LANES), lambda i: (i, 0)),
                   pl.BlockSpec((8, LANES), lambda i: (0, 0))],
        out_shape=[jax.ShapeDtypeStruct((n_rows, d), F32),
                   jax.ShapeDtypeStruct((n_rows, LANES), F32),
                   jax.ShapeDtypeStruct((8, LANES), F32)],
        scratch_shapes=[pltpu.VMEM((8, LANES), F32)],
        compiler_params=_cparams(("arbitrary",)),
        name="moe_router",
    )(xs, g.reshape(1, d), mod, wr_pad)


MOE_TB = 1024
MOE_SUB = 512
MOE_TF = 256
DMA_UNROLL = 8


def _dispatch_kernel(d1_ref, d2_ref, h_ref, xs_in_ref, xs_ref, sem, *, tm):
    del xs_in_ref

    def issue(t, carry):
        src = h_ref.at[pl.ds(t, 1), :]
        pltpu.make_async_copy(src, xs_ref.at[pl.ds(d1_ref[0, 0, t], 1), :], sem).start()
        pltpu.make_async_copy(src, xs_ref.at[pl.ds(d2_ref[0, 0, t], 1), :], sem).start()
        return carry

    lax.fori_loop(0, tm, issue, 0, unroll=DMA_UNROLL)

    def drain(t, carry):
        cp = pltpu.make_async_copy(h_ref.at[pl.ds(0, 1), :], xs_ref.at[pl.ds(0, 1), :], sem)
        cp.wait()
        cp.wait()
        return carry

    lax.fori_loop(0, tm, drain, 0, unroll=DMA_UNROLL)


def _dispatch_call(h, dest1, dest2, n_slots):
    n, d = h.shape
    tm = 256
    kern = functools.partial(_dispatch_kernel, tm=tm)
    idx = pl.BlockSpec((1, 1, tm), lambda i: (i, 0, 0), memory_space=pltpu.SMEM)
    anyspec = pl.BlockSpec(memory_space=pl.ANY)
    return pl.pallas_call(
        kern,
        grid=(n // tm,),
        in_specs=[idx, idx, pl.BlockSpec((tm, d), lambda i: (i, 0)), anyspec],
        out_specs=anyspec,
        out_shape=jax.ShapeDtypeStruct((n_slots, d), F32),
        scratch_shapes=[pltpu.SemaphoreType.DMA(())],
        input_output_aliases={3: 0},
        compiler_params=_cparams(("arbitrary",)),
        name="moe_dispatch",
    )(dest1.reshape(n // tm, 1, tm), dest2.reshape(n // tm, 1, tm), h, jnp.zeros((n_slots, d), F32))


def _grouped_ffn_kernel(be_ref, nv_ref, rv_ref, x_ref, w1_ref, w3_ref, w2_ref, o_ref, xb_ref):
    del be_ref, nv_ref
    b = pl.program_id(0)
    k = pl.program_id(1)
    rows = rv_ref[b]

    @pl.when(k == 0)
    def _():
        xb_ref[...] = x_ref[...].astype(BF16)
        o_ref[...] = jnp.zeros_like(o_ref)

    def swiglu_rows(n):
        w1 = w1_ref[0].astype(BF16)
        w3 = w3_ref[0].astype(BF16)
        w2 = w2_ref[0].astype(BF16)
        xb = xb_ref[0:n, :]
        u1 = jnp.dot(xb, w1, preferred_element_type=F32)
        u3 = jnp.dot(xb, w3, preferred_element_type=F32)
        u = (u1 * _sigmoid(u1)) * u3
        o_ref[0:n, :] += jnp.dot(u.astype(BF16), w2, preferred_element_type=F32)

    pl.when(rows > MOE_SUB)(lambda: swiglu_rows(MOE_TB))
    pl.when(jnp.logical_and(rows > 0, rows <= MOE_SUB))(lambda: swiglu_rows(MOE_SUB))


def _grouped_ffn_call(xs, w1, w3, w2, block_expert, n_valid, rows_valid):
    n_slots, d = xs.shape
    f = w1.shape[2]
    tb, tf = MOE_TB, MOE_TF
    n_k = f // tf
    assert n_slots % tb == 0 and f % tf == 0

    def blk(b, nv):
        return jnp.minimum(b, nv[0] - 1)

    def kk(b, k, nv):
        return jnp.where(b < nv[0], k, n_k - 1)

    grid_spec = pltpu.PrefetchScalarGridSpec(
        num_scalar_prefetch=3,
        grid=(n_slots // tb, n_k),
        in_specs=[pl.BlockSpec((tb, d), lambda b, k, be, nv, rv: (blk(b, nv), 0)),
                  pl.BlockSpec((1, d, tf), lambda b, k, be, nv, rv: (be[blk(b, nv)], 0, kk(b, k, nv))),
                  pl.BlockSpec((1, d, tf), lambda b, k, be, nv, rv: (be[blk(b, nv)], 0, kk(b, k, nv))),
                  pl.BlockSpec((1, tf, d), lambda b, k, be, nv, rv: (be[blk(b, nv)], kk(b, k, nv), 0))],
        out_specs=pl.BlockSpec((tb, d), lambda b, k, be, nv, rv: (b, 0)),
        scratch_shapes=[pltpu.VMEM((tb, d), BF16)])
    return pl.pallas_call(
        _grouped_ffn_kernel,
        grid_spec=grid_spec,
        out_shape=jax.ShapeDtypeStruct((n_slots, d), F32),
        compiler_params=_cparams(("arbitrary", "arbitrary")),
        name="moe_grouped_ffn",
    )(block_expert, n_valid, rows_valid, xs, w1, w3, w2)


def _combine_kernel(d1_ref, d2_ref, route_ref, x_ref, g3_ref, mod_ref, y_ref, o_ref, ya_ref, yb_ref, sem,
                    *, tm, n_rows, n_lat):
    i = pl.program_id(0)

    def issue(t, carry):
        pltpu.make_async_copy(y_ref.at[pl.ds(d1_ref[0, 0, t], 1), :], ya_ref.at[pl.ds(t, 1), :], sem).start()
        pltpu.make_async_copy(y_ref.at[pl.ds(d2_ref[0, 0, t], 1), :], yb_ref.at[pl.ds(t, 1), :], sem).start()
        return carry

    lax.fori_loop(0, tm, issue, 0, unroll=DMA_UNROLL)

    def drain(t, carry):
        cp = pltpu.make_async_copy(y_ref.at[pl.ds(0, 1), :], ya_ref.at[pl.ds(0, 1), :], sem)
        cp.wait()
        cp.wait()
        return carry

    lax.fori_loop(0, tm, drain, 0, unroll=DMA_UNROLL)

    r = route_ref[...]
    f = r[:, 2:3] * ya_ref[...] + r[:, 3:4] * yb_ref[...]
    if n_rows > n_lat:
        gate = jnp.where(i * tm >= n_lat, mod_ref[5:6, :], mod_ref[2:3, :])
    else:
        gate = mod_ref[2:3, :]
    o_ref[...] = x_ref[...] + gate * _rms(f, g3_ref[...])


def _combine_call(y, dest1, dest2, route, xs, g3, mod, n_rows, n_lat):
    d = xs.shape[1]
    tm = 256
    kern = functools.partial(_combine_kernel, tm=tm, n_rows=n_rows, n_lat=n_lat)
    idx = pl.BlockSpec((1, 1, tm), lambda i: (i, 0, 0), memory_space=pltpu.SMEM)
    return pl.pallas_call(
        kern,
        grid=(n_rows // tm,),
        in_specs=[idx, idx,
                  pl.BlockSpec((tm, LANES), lambda i: (i, 0)),
                  pl.BlockSpec((tm, d), lambda i: (i, 0)),
                  pl.BlockSpec((1, d), lambda i: (0, 0)),
                  pl.BlockSpec((8, d), lambda i: (0, 0)),
                  pl.BlockSpec(memory_space=pl.ANY)],
        out_specs=pl.BlockSpec((tm, d), lambda i: (i, 0)),
        out_shape=jax.ShapeDtypeStruct((n_rows, d), F32),
        scratch_shapes=[pltpu.VMEM((tm, d), F32), pltpu.VMEM((tm, d), F32), pltpu.SemaphoreType.DMA(())],
        compiler_params=_cparams(("arbitrary",)),
        name="moe_combine",
    )(dest1.reshape(n_rows // tm, 1, tm), dest2.reshape(n_rows // tm, 1, tm), route, xs, g3.reshape(1, d), mod, y)


def _moe_call(xs, g2, mod, wr_pad, w1, w3, w2, li, g3, n_rows, n_lat):
    h, route, cnt = _router_call(xs, g2, mod, wr_pad, n_rows, n_lat)
    tb = MOE_TB
    n_slots = -(-(2 * n_rows + N_EXPERTS * (tb - 1)) // tb) * tb
    counts = cnt[0, :N_EXPERTS].astype(jnp.int32)
    padded = (counts + tb - 1) // tb * tb
    ends = jnp.cumsum(padded)
    offs = ends - padded
    e1, e2 = route[:, 0].astype(jnp.int32), route[:, 1].astype(jnp.int32)
    dest1 = offs[e1] + route[:, 4].astype(jnp.int32)
    dest2 = offs[e2] + route[:, 5].astype(jnp.int32)
    blk_start = jnp.arange(n_slots // tb, dtype=jnp.int32) * tb
    block_expert = jnp.minimum(jnp.sum(blk_start[:, None] >= ends[None, :], axis=1), N_EXPERTS - 1)
    n_valid = (ends[-1:] // tb).astype(jnp.int32)
    rows_valid = jnp.clip((offs + counts)[block_expert] - blk_start, 0, tb).astype(jnp.int32)
    rows_valid = jnp.where(blk_start < ends[-1], rows_valid, 0)
    xg = _dispatch_call(h, dest1, dest2, n_slots)
    slab = (block_expert + li * N_EXPERTS).astype(jnp.int32)
    y = _grouped_ffn_call(xg, w1, w3, w2, slab, n_valid, rows_valid)
    return _combine_call(y, dest1, dest2, route, xs, g3, mod, n_rows, n_lat)


def _ffn_kernel(x_ref, g2_ref, mod_ref, w1_ref, w3_ref, w2_ref, g3_ref, o_ref, h_ref, acc_ref,
                *, tm, n_rows, n_lat, n_k):
    i = pl.program_id(0)
    k = pl.program_id(1)

    @pl.when(k == 0)
    def _():
        def fill(r0, r1, is_ctx):
            r = 3 if is_ctx else 0
            h = _rms(x_ref[r0:r1, :], g2_ref[...]) * (1.0 + mod_ref[r:r + 1, :]) + mod_ref[r + 1:r + 2, :]
            h_ref[r0:r1, :] = h.astype(BF16)
        _row_groups(i, tm, n_rows, n_lat, fill)
        acc_ref[...] = jnp.zeros_like(acc_ref)

    h = h_ref[...]
    u1 = jnp.dot(h, w1_ref[0], preferred_element_type=F32)
    u3 = jnp.dot(h, w3_ref[0], preferred_element_type=F32)
    u = (u1 * _sigmoid(u1)) * u3
    acc_ref[...] += jnp.dot(u.astype(BF16), w2_ref[0], preferred_element_type=F32)

    @pl.when(k == n_k - 1)
    def _():
        def fin(r0, r1, is_ctx):
            r = 5 if is_ctx else 2
            fn = _rms(acc_ref[r0:r1, :], g3_ref[...])
            o_ref[r0:r1, :] = x_ref[r0:r1, :] + mod_ref[r:r + 1, :] * fn
        _row_groups(i, tm, n_rows, n_lat, fin)


def _ffn_call(xs, g2, mod, w1, w3, w2, li, g3, n_rows, n_lat):
    d = xs.shape[1]
    f = w1.shape[2]
    tm = _pick(n_rows, (768, 512, 256))
    tf = _pick(f, (512, 256, 128))
    n_k = f // tf
    kern = functools.partial(_ffn_kernel, tm=tm, n_rows=n_rows, n_lat=n_lat, n_k=n_k)
    return pl.pallas_call(
        kern,
        grid=(n_rows // tm, n_k),
        in_specs=[pl.BlockSpec((tm, d), lambda i, k: (i, 0)),
                  pl.BlockSpec((1, d), lambda i, k: (0, 0)),
                  pl.BlockSpec((8, d), lambda i, k: (0, 0)),
                  pl.BlockSpec((1, d, tf), lambda i, k: (li, 0, k)),
                  pl.BlockSpec((1, d, tf), lambda i, k: (li, 0, k)),
                  pl.BlockSpec((1, tf, d), lambda i, k: (li, k, 0)),
                  pl.BlockSpec((1, d), lambda i, k: (0, 0))],
        out_specs=pl.BlockSpec((tm, d), lambda i, k: (i, 0)),
        out_shape=jax.ShapeDtypeStruct((n_rows, d), F32),
        scratch_shapes=[pltpu.VMEM((tm, d), BF16), pltpu.VMEM((tm, d), F32)],
        compiler_params=_cparams(("arbitrary", "arbitrary")),
        name="dense_ffn",
    )(xs, g2.reshape(1, d), mod, w1, w3, w2, g3.reshape(1, d))


def _prep_w_in(w_in):
    o_kr, o_dq = MLA_Q_RANK + MLA_KV_RANK, MLA_Q_RANK + MLA_KV_RANK + MLA_ROPE
    pad = jnp.zeros(w_in.shape[:-1] + (D_IN_PAD - w_in.shape[-1],), w_in.dtype)
    return jnp.concatenate([w_in[..., :o_kr], w_in[..., o_dq:], w_in[..., o_kr:o_dq], pad], axis=-1).astype(BF16)


def _prep_w_uq(w_uq):
    lead = w_uq.shape[:-1]
    w = w_uq.reshape(lead + (MLA_HEADS, MLA_NOPE + MLA_ROPE))
    w = jnp.pad(w, [(0, 0)] * len(lead) + [(0, 0), (0, MLA_QK_PAD - MLA_NOPE - MLA_ROPE)])
    return w.reshape(lead + (MLA_HEADS * MLA_QK_PAD,)).astype(BF16)


def _prep_w_ukv(w_ukv):
    lead = w_ukv.shape[:-1]
    w = w_ukv.reshape(lead + (MLA_HEADS, MLA_NOPE + MLA_V))
    wk = w[..., :MLA_NOPE].reshape(lead + (MLA_HEADS * MLA_NOPE,))
    wv = w[..., MLA_NOPE:].reshape(lead + (MLA_HEADS * MLA_V,))
    return wk.astype(BF16), wv.astype(BF16)


def _rope_tables(n_lat, n_ctx):
    t = jnp.arange(n_lat, dtype=jnp.int32)
    row = (t // GRID_W).astype(F32)[:, None]
    col = (t % GRID_W).astype(F32)[:, None]

    def table(half):
        freqs = ROPE_BASE ** (-jnp.arange(half, dtype=F32) / half)
        ar, ac = row * freqs, col * freqs
        z = jnp.zeros((n_lat, half), F32)
        pad = jnp.zeros((n_lat, LANES - 4 * half), F32)
        cos = jnp.concatenate([jnp.cos(ar), jnp.cos(ar), jnp.cos(ac), jnp.cos(ac), 1.0 + pad], axis=-1)
        sin_lo = jnp.concatenate([-jnp.sin(ar), z, -jnp.sin(ac), z, pad], axis=-1)
        sin_hi = jnp.concatenate([z, jnp.sin(ar), z, jnp.sin(ac), pad], axis=-1)
        ident = jnp.ones((n_ctx, LANES), F32), jnp.zeros((n_ctx, LANES), F32), jnp.zeros((n_ctx, LANES), F32)
        return tuple(jnp.concatenate([a, b], axis=0) for a, b in zip((cos, sin_lo, sin_hi), ident))

    return table(DIFF_QK // 4) + table(MLA_ROPE // 4)


def kernel(x, c, ctx, c_ctx, w_mod, b_mod, g_norm, w_in, w_uq, g_qn, w_ukv, g_kvn, lam, g_sub, w_po_mla,
           w_po_diff, w_out, w1_dense, w3_dense, w2_dense, w_router, w1_moe, w3_moe, w2_moe):
    b, s, d = x.shape
    n_ctx = ctx.shape[1]
    depth = w_mod.shape[0]
    assert b == 1 and d == D_MODEL and s % GRID_W == 0
    n_tok = s + n_ctx

    xs = jnp.concatenate([x[0], ctx[0]], axis=0)
    c8 = jnp.concatenate([c, c_ctx[None, :], jnp.zeros((6, d), F32)], axis=0)
    mods = _mod_call(c8, w_mod, b_mod)

    w_in_p = _prep_w_in(w_in)
    w_uq_p = _prep_w_uq(w_uq)
    w_k_p, w_v_p = _prep_w_ukv(w_ukv)
    w_pm, w_pd, w_o = w_po_mla.astype(BF16), w_po_diff.astype(BF16), w_out.astype(BF16)
    w1d, w3d, w2d = w1_dense.astype(BF16), w3_dense.astype(BF16), w2_dense.astype(BF16)
    wr_pad = jnp.pad(w_router, ((0, 0), (0, 0), (0, LANES - N_EXPERTS)))
    flat = lambda w: w.reshape((-1,) + w.shape[2:])
    w1m, w3m, w2m = flat(w1_moe), flat(w3_moe), flat(w2_moe)
    tabs = _rope_tables(s, n_ctx)
    zrow = jnp.zeros((d,), F32)

    for layer in range(depth):
        need_ctx = layer < depth - 1
        lat, cx = mods[layer, 0], mods[layer, 1]
        seg = lambda v, k: v[k * d:(k + 1) * d]
        gn = g_norm[layer]

        mod1 = jnp.stack([seg(lat, 1), seg(lat, 0), seg(cx, 1), seg(cx, 0), zrow, zrow, zrow, zrow])
        proj = _norm_matmul_call(xs, gn[0], mod1, w_in_p, layer, s)
        qm, km, vm, qd, kd = _prep_call(proj, g_qn[layer].reshape(1, -1), g_kvn[layer].reshape(1, -1),
                                        w_uq_p, w_k_p, w_v_p, layer, tabs)
        lam_init = 0.8 - 0.6 * math.exp(-0.3 * layer)
        o_mla = _mla_attn_call(qm, km, vm, 0, s, 0, n_tok)
        o_d = _diff_attn_call(qd, kd, proj, lam[layer], g_sub[layer], lam_init, 0, s, 0, n_tok)
        o_mla_c = o_d_c = None
        if need_ctx:
            o_mla_c = _mla_attn_call(qm, km, vm, s, n_ctx, s, n_ctx)
            o_d_c = _diff_attn_call(qd, kd, proj, lam[layer], g_sub[layer], lam_init, s, n_ctx, s, n_ctx)
        n_rows = n_tok if need_ctx else s
        gate1 = jnp.stack([seg(lat, 2), seg(cx, 2), zrow, zrow, zrow, zrow, zrow, zrow])
        xs = _post_call(o_mla, o_d, o_mla_c, o_d_c, proj, xs, w_pm, w_pd, w_o, layer, gn[1], gate1, s)

        mod2 = jnp.stack([seg(lat, 4), seg(lat, 3), seg(lat, 5), seg(cx, 4), seg(cx, 3), seg(cx, 5), zrow, zrow])
        i = layer // 2
        if layer % 2 == 0:
            xs = _ffn_call(xs, gn[2], mod2, w1d, w3d, w2d, i, gn[3], n_rows, s)
        else:
            xs = _moe_call(xs, gn[2], mod2, wr_pad[i], w1m, w3m, w2m, i, gn[3], n_rows, s)
    return xs[:s][None]
```

```python
import functools
import math

import jax
import jax.numpy as jnp
from jax import lax
from jax.experimental import pallas as pl
from jax.experimental.pallas import tpu as pltpu

F32 = jnp.float32
BF16 = jnp.bfloat16

D_MODEL = 2048
GRID_W = 64
MLA_HEADS = 8
MLA_Q_RANK = 512
MLA_KV_RANK = 512
MLA_NOPE = 128
MLA_ROPE = 64
MLA_V = 128
MLA_QK_PAD = 256
DIFF_HEADS = 4
DIFF_QK = 128
DIFF_V = 256
N_EXPERTS = 8
ROPE_BASE = 10000.0
NORM_EPS = 1e-6
LANES = 128
LOG2E = math.log2(math.e)

_OFF_CQ, _OFF_CKV, _OFF_DQ, _OFF_DK, _OFF_DV, _OFF_GA, _OFF_GB, _OFF_KR = (
    0, 512, 1024, 2048, 3072, 4096, 6144, 8192)
D_IN_PAD = 8448

VMEM_LIMIT = 56 * 1024 * 1024


def _pick(n, cands):
    for c in cands:
        if n % c == 0:
            return c
    raise ValueError(f"no tile for {n} in {cands}")


def _cparams(sem):
    return pltpu.CompilerParams(dimension_semantics=sem, vmem_limit_bytes=VMEM_LIMIT)


def _rms(xf, g):
    ms = jnp.mean(xf * xf, axis=-1, keepdims=True)
    return xf * lax.rsqrt(ms + NORM_EPS) * g


def _sigmoid(v):
    return 1.0 / (1.0 + jnp.exp(-v))


def _row_groups(i, tm, n_rows, n_lat, fn):
    if n_rows <= n_lat:
        fn(0, tm, False)
        return
    nb = n_rows // tm
    b, off = divmod(n_lat, tm)
    if b > 0:
        pl.when(i < b)(lambda: fn(0, tm, False))
    if off:
        def _split():
            fn(0, off, False)
            fn(off, tm, True)
        pl.when(i == b)(_split)
        if nb > b + 1:
            pl.when(i > b)(lambda: fn(0, tm, True))
    else:
        pl.when(i >= b)(lambda: fn(0, tm, True))


def _mod_kernel(c_ref, w_ref, b_ref, o_ref):
    cv = c_ref[...]
    s = cv * _sigmoid(cv)
    o_ref[0] = jnp.dot(s, w_ref[0], preferred_element_type=F32,
                       precision=lax.Precision.HIGHEST) + b_ref[0]


def _mod_call(c8, w_mod, b_mod):
    depth, d, n6 = w_mod.shape
    tn = _pick(n6, (1536, 1024, 512, 256, 128))
    return pl.pallas_call(
        _mod_kernel,
        grid=(depth, n6 // tn),
        in_specs=[pl.BlockSpec((8, d), lambda l, j: (0, 0)),
                  pl.BlockSpec((1, d, tn), lambda l, j: (l, 0, j)),
                  pl.BlockSpec((1, 1, tn), lambda l, j: (l, 0, j))],
        out_specs=pl.BlockSpec((1, 8, tn), lambda l, j: (l, 0, j)),
        out_shape=jax.ShapeDtypeStruct((depth, 8, n6), F32),
        compiler_params=_cparams(("arbitrary", "arbitrary")),
        name="adaln_mod",
    )(c8, w_mod, b_mod.reshape(depth, 1, n6))


def _norm_matmul_kernel(x_ref, g_ref, mod_ref, w_ref, o_ref, h_ref, *, tm, n_rows, n_lat):
    i = pl.program_id(0)
    j = pl.program_id(1)

    @pl.when(j == 0)
    def _():
        def fill(r0, r1, is_ctx):
            r = 2 if is_ctx else 0
            h = _rms(x_ref[r0:r1, :], g_ref[...]) * (1.0 + mod_ref[r:r + 1, :]) + mod_ref[r + 1:r + 2, :]
            h_ref[r0:r1, :] = h.astype(BF16)
        _row_groups(i, tm, n_rows, n_lat, fill)

    o_ref[...] = jnp.dot(h_ref[...], w_ref[0], preferred_element_type=F32).astype(o_ref.dtype)


def _norm_matmul_call(xs, g, mod, w, layer, n_lat):
    n_rows, d = xs.shape
    n_out = w.shape[2]
    tm = _pick(n_rows, (768, 512, 256))
    tn = _pick(n_out, (1408, 1024, 768, 512, 256))
    kern = functools.partial(_norm_matmul_kernel, tm=tm, n_rows=n_rows, n_lat=n_lat)
    return pl.pallas_call(
        kern,
        grid=(n_rows // tm, n_out // tn),
        in_specs=[pl.BlockSpec((tm, d), lambda i, j: (i, 0)),
                  pl.BlockSpec((1, d), lambda i, j: (0, 0)),
                  pl.BlockSpec((8, d), lambda i, j: (0, 0)),
                  pl.BlockSpec((1, d, tn), lambda i, j: (layer, 0, j))],
        out_specs=pl.BlockSpec((tm, tn), lambda i, j: (i, j)),
        out_shape=jax.ShapeDtypeStruct((n_rows, n_out), BF16),
        scratch_shapes=[pltpu.VMEM((tm, d), BF16)],
        compiler_params=_cparams(("arbitrary", "arbitrary")),
        name="norm_in_proj",
    )(xs, g.reshape(1, d), mod, w)


def _rope(xf, tab, half):
    cos, sin_lo, sin_hi = tab
    return xf * cos + pltpu.roll(xf, LANES - half, 1) * sin_lo + pltpu.roll(xf, half, 1) * sin_hi


def _prep_kernel(cq_ref, ckv_ref, dq_ref, dk_ref, kr_ref, gq_ref, gkv_ref, wuq_ref, wk_ref, wv_ref,
                 cosd_ref, sld_ref, shd_ref, cosm_ref, slm_ref, shm_ref,
                 qm_ref, km_ref, vm_ref, qd_ref, kd_ref, *, s_mla, s_diff):
    tabm = (cosm_ref[...], slm_ref[...], shm_ref[...])
    tabd = (cosd_ref[...], sld_ref[...], shd_ref[...])
    half_m, half_d = MLA_ROPE // 4, DIFF_QK // 4

    cqn = _rms(cq_ref[...].astype(F32), gq_ref[...]).astype(BF16)
    q = jnp.dot(cqn, wuq_ref[0], preferred_element_type=F32)
    for h in range(MLA_HEADS):
        base = h * MLA_QK_PAD
        qm_ref[h, :, 0:LANES] = (q[:, base:base + LANES] * s_mla).astype(BF16)
        qr = _rope(q[:, base + LANES:base + 2 * LANES], tabm, half_m)
        qm_ref[h, :, LANES:2 * LANES] = (qr * s_mla).astype(BF16)

    ckvn = _rms(ckv_ref[...].astype(F32), gkv_ref[...]).astype(BF16)
    kn = jnp.dot(ckvn, wk_ref[0], preferred_element_type=F32)
    vv = jnp.dot(ckvn, wv_ref[0], preferred_element_type=F32)
    kr = _rope(kr_ref[...].astype(F32), tabm, half_m).astype(BF16)
    for h in range(MLA_HEADS):
        km_ref[h, :, 0:LANES] = kn[:, h * LANES:(h + 1) * LANES].astype(BF16)
        km_ref[h, :, LANES:2 * LANES] = kr
        vm_ref[h] = vv[:, h * LANES:(h + 1) * LANES].astype(BF16)

    for hc in range(2 * DIFF_HEADS):
        xq = dq_ref[:, hc * LANES:(hc + 1) * LANES].astype(F32)
        qd_ref[hc] = (_rope(xq, tabd, half_d) * s_diff).astype(BF16)
        xk = dk_ref[:, hc * LANES:(hc + 1) * LANES].astype(F32)
        kd_ref[hc] = _rope(xk, tabd, half_d).astype(BF16)


def _prep_call(proj, gq, gkv, wuq, wk, wv, layer, tabs):
    n = proj.shape[0]
    tm = _pick(n, (384, 256, 128))
    nh, nd = MLA_HEADS, 2 * DIFF_HEADS
    kern = functools.partial(_prep_kernel, s_mla=float((MLA_NOPE + MLA_ROPE) ** -0.5 * LOG2E),
                             s_diff=float(DIFF_QK ** -0.5 * LOG2E))
    row = lambda w, c: pl.BlockSpec((tm, w), lambda i, c=c: (i, c))
    const = lambda a: pl.BlockSpec(a.shape, lambda i: (0,) * a.ndim)
    wspec = lambda a: pl.BlockSpec((1,) + a.shape[1:], lambda i: (layer, 0, 0))
    tab = pl.BlockSpec((tm, LANES), lambda i: (i, 0))
    return pl.pallas_call(
        kern,
        grid=(n // tm,),
        in_specs=[row(512, _OFF_CQ // 512), row(512, _OFF_CKV // 512), row(1024, _OFF_DQ // 1024),
                  row(1024, _OFF_DK // 1024), row(LANES, _OFF_KR // LANES),
                  const(gq), const(gkv), wspec(wuq), wspec(wk), wspec(wv)] + [tab] * 6,
        out_specs=[pl.BlockSpec((nh, tm, MLA_QK_PAD), lambda i: (0, i, 0)),
                   pl.BlockSpec((nh, tm, MLA_QK_PAD), lambda i: (0, i, 0)),
                   pl.BlockSpec((nh, tm, MLA_V), lambda i: (0, i, 0)),
                   pl.BlockSpec((nd, tm, DIFF_QK), lambda i: (0, i, 0)),
                   pl.BlockSpec((nd, tm, DIFF_QK), lambda i: (0, i, 0))],
        out_shape=[jax.ShapeDtypeStruct((nh, n, MLA_QK_PAD), BF16),
                   jax.ShapeDtypeStruct((nh, n, MLA_QK_PAD), BF16),
                   jax.ShapeDtypeStruct((nh, n, MLA_V), BF16),
                   jax.ShapeDtypeStruct((nd, n, DIFF_QK), BF16),
                   jax.ShapeDtypeStruct((nd, n, DIFF_QK), BF16)],
        compiler_params=_cparams(("arbitrary",)),
        name="qkv_prep",
    )(proj, proj, proj, proj, proj, gq, gkv, wuq, wk, wv, *tabs)


def _softmax_pv(q, k_at, v_at, tk, n_chunks, dv):
    tq = q.shape[0]
    m = jnp.full((tq, 1), -jnp.inf, F32)
    l = jnp.zeros((tq, 1), F32)
    acc = jnp.zeros((tq, dv), F32)
    for c in range(n_chunks):
        off = c * tk
        s = lax.dot_general(q, k_at(off), (((1,), (1,)), ((), ())), preferred_element_type=F32)
        m_new = jnp.maximum(m, jnp.max(s, axis=-1, keepdims=True))
        alpha = jnp.exp2(m - m_new)
        p = jnp.exp2(s - m_new)
        l = alpha * l + jnp.sum(p, axis=-1, keepdims=True)
        acc = alpha * acc + jnp.dot(p.astype(BF16), v_at(off), preferred_element_type=F32)
        m = m_new
    return acc / l


def _mla_attn_kernel(q_ref, k_ref, v_ref, o_ref, *, tk, n_chunks, n_sub):
    ts = q_ref.shape[1] // n_sub
    qs = [q_ref[0, i * ts:(i + 1) * ts, :] for i in range(n_sub)]
    m = [jnp.full((ts, 1), -jnp.inf, F32) for _ in range(n_sub)]
    l = [jnp.zeros((ts, 1), F32) for _ in range(n_sub)]
    acc = [jnp.zeros((ts, MLA_V), F32) for _ in range(n_sub)]
    for c in range(n_chunks):
        k = k_ref[0, c * tk:(c + 1) * tk, :]
        v = v_ref[0, c * tk:(c + 1) * tk, :]
        for i in range(n_sub):
            s = lax.dot_general(qs[i], k, (((1,), (1,)), ((), ())), preferred_element_type=F32)
            m_new = jnp.maximum(m[i], jnp.max(s, axis=-1, keepdims=True))
            alpha = jnp.exp2(m[i] - m_new)
            p = jnp.exp2(s - m_new)
            l[i] = alpha * l[i] + jnp.sum(p, axis=-1, keepdims=True)
            acc[i] = alpha * acc[i] + jnp.dot(p.astype(BF16), v, preferred_element_type=F32)
            m[i] = m_new
    for i in range(n_sub):
        o_ref[i * ts:(i + 1) * ts, :] = (acc[i] / l[i]).astype(o_ref.dtype)


def _mla_attn_call(qm, km, vm, q_row0, n_q, kv_row0, n_kv):
    tq = _pick(n_q, (1024, 512, 256, 128))
    tk = _pick(n_kv, (2816, 768, 512, 256, 128))
    assert q_row0 % tq == 0 and kv_row0 % n_kv == 0
    qb, kb = q_row0 // tq, kv_row0 // n_kv
    kern = functools.partial(_mla_attn_kernel, tk=tk, n_chunks=n_kv // tk, n_sub=max(1, tq // 512))
    return pl.pallas_call(
        kern,
        grid=(MLA_HEADS, n_q // tq),
        in_specs=[pl.BlockSpec((1, tq, MLA_QK_PAD), lambda h, i: (h, i + qb, 0)),
                  pl.BlockSpec((1, n_kv, MLA_QK_PAD), lambda h, i: (h, kb, 0)),
                  pl.BlockSpec((1, n_kv, MLA_V), lambda h, i: (h, kb, 0))],
        out_specs=pl.BlockSpec((tq, MLA_V), lambda h, i: (i, h)),
        out_shape=jax.ShapeDtypeStruct((n_q, MLA_HEADS * MLA_V), BF16),
        compiler_params=_cparams(("arbitrary", "arbitrary")),
        name="mla_attention",
    )(qm, km, vm)


def _diff_attn_kernel(q_ref, k_ref, v_ref, lam_ref, gs_ref, o_ref, *, tk, n_chunks):
    lam_init = lam_ref[4:5, 0:1]
    outs = []
    for comp in range(2):
        outs.append(_softmax_pv(q_ref[comp],
                                lambda off, comp=comp: k_ref[comp, pl.ds(off, tk), :],
                                lambda off: v_ref[pl.ds(off, tk), :],
                                tk, n_chunks, DIFF_V))
    lf = lam_ref[...]
    lam_full = (jnp.exp(jnp.sum(lf[0:1] * lf[1:2], axis=-1, keepdims=True))
                - jnp.exp(jnp.sum(lf[2:3] * lf[3:4], axis=-1, keepdims=True)) + lam_init)
    d = outs[0] - lam_full * outs[1]
    o_ref[...] = (_rms(d, gs_ref[...]) * (1.0 - lam_init)).astype(o_ref.dtype)


def _diff_attn_call(qd, kd, proj, lam, gsub, lam_init, q_row0, n_q, kv_row0, n_kv):
    tq = _pick(n_q, (512, 256, 128))
    tk = _pick(n_kv, (2816, 768, 512, 256, 128))
    assert q_row0 % tq == 0 and kv_row0 % n_kv == 0
    qb, kb = q_row0 // tq, kv_row0 // n_kv
    vcol = _OFF_DV // DIFF_V
    kern = functools.partial(_diff_attn_kernel, tk=tk, n_chunks=n_kv // tk)
    in_specs = [pl.BlockSpec((2, tq, DIFF_QK), lambda h, i: (h, i + qb, 0)),
                pl.BlockSpec((2, n_kv, DIFF_QK), lambda h, i: (h, kb, 0)),
                pl.BlockSpec((n_kv, DIFF_V), lambda h, i: (kb, vcol + h)),
                pl.BlockSpec((8, DIFF_QK), lambda h, i: (0, 0)),
                pl.BlockSpec((1, DIFF_V), lambda h, i: (0, 0))]
    lam8 = jnp.concatenate([lam, jnp.full((4, DIFF_QK), lam_init, F32)], axis=0)
    return pl.pallas_call(
        kern,
        grid=(DIFF_HEADS, n_q // tq),
        in_specs=in_specs,
        out_specs=pl.BlockSpec((tq, DIFF_V), lambda h, i: (i, h)),
        out_shape=jax.ShapeDtypeStruct((n_q, DIFF_HEADS * DIFF_V), BF16),
        compiler_params=_cparams(("arbitrary", "arbitrary")),
        name="diff_attention",
    )(qd, kd, proj, lam8, gsub.reshape(1, DIFF_V))


def _post_kernel(*refs, tm, n_lat, with_ctx):
    if with_ctx:
        (om_ref, od_ref, omc_ref, odc_ref, ga_ref, gb_ref, x_ref, wpm_ref, wpd_ref, wo_ref, g_ref, gate_ref,
         o_ref) = refs
    else:
        om_ref, od_ref, ga_ref, gb_ref, x_ref, wpm_ref, wpd_ref, wo_ref, g_ref, gate_ref, o_ref = refs
    i = pl.program_id(0)
    om, od, gate = om_ref[...], od_ref[...], gate_ref[0:1, :]
    if with_ctx:
        is_ctx = i * tm >= n_lat
        om = jnp.where(is_ctx, omc_ref[...], om)
        od = jnp.where(is_ctx, odc_ref[...], od)
        gate = jnp.where(is_ctx, gate_ref[1:2, :], gate)
    a = jnp.dot(om, wpm_ref[0], preferred_element_type=F32)
    b = jnp.dot(od, wpd_ref[0], preferred_element_type=F32)
    y = _sigmoid(ga_ref[...].astype(F32)) * a + _sigmoid(gb_ref[...].astype(F32)) * b
    z = jnp.dot(y.astype(BF16), wo_ref[0], preferred_element_type=F32)
    o_ref[...] = x_ref[...] + gate * _rms(z, g_ref[...])


def _post_call(o_mla, o_d, o_mla_ctx, o_d_ctx, proj, xs, wpm, wpd, wo, layer, g, gate, n_lat):
    d = xs.shape[1]
    tm = 256
    with_ctx = o_mla_ctx is not None
    n_rows = n_lat + (o_mla_ctx.shape[0] if with_ctx else 0)
    assert n_rows % tm == 0 and n_lat % tm == 0
    nb_lat = n_lat // tm
    kern = functools.partial(_post_kernel, tm=tm, n_lat=n_lat, with_ctx=with_ctx)
    const = lambda a: pl.BlockSpec((1,) + a.shape[1:], lambda i: (layer, 0, 0), pipeline_mode=pl.Buffered(1))
    lat = lambda a: pl.BlockSpec((tm, a.shape[1]), lambda i: (jnp.minimum(i, nb_lat - 1), 0))
    cxs = lambda a: pl.BlockSpec((tm, a.shape[1]), lambda i: (jnp.maximum(i - nb_lat, 0), 0))
    in_specs = [lat(o_mla), lat(o_d)]
    args = [o_mla, o_d]
    if with_ctx:
        in_specs += [cxs(o_mla_ctx), cxs(o_d_ctx)]
        args += [o_mla_ctx, o_d_ctx]
    in_specs += [pl.BlockSpec((tm, d), lambda i: (i, _OFF_GA // D_MODEL)),
                 pl.BlockSpec((tm, d), lambda i: (i, _OFF_GB // D_MODEL)),
                 pl.BlockSpec((tm, d), lambda i: (i, 0)),
                 const(wpm), const(wpd), const(wo),
                 pl.BlockSpec((1, d), lambda i: (0, 0)),
                 pl.BlockSpec((8, d), lambda i: (0, 0))]
    args += [proj, proj, xs, wpm, wpd, wo, g.reshape(1, d), gate]
    return pl.pallas_call(
        kern,
        grid=(n_rows // tm,),
        in_specs=in_specs,
        out_specs=pl.BlockSpec((tm, d), lambda i: (i, 0)),
        out_shape=jax.ShapeDtypeStruct((n_rows, d), F32),
        compiler_params=_cparams(("arbitrary",)),
        name="merge_out_proj",
    )(*args)


def _router_kernel(x_ref, g_ref, mod_ref, wr_ref, h_ref, route_ref, cnt_ref, carry_ref, *, tm, n_rows, n_lat):
    i = pl.program_id(0)

    @pl.when(i == 0)
    def _():
        carry_ref[...] = jnp.zeros_like(carry_ref)

    if n_rows > n_lat:
        is_ctx = i * tm >= n_lat
        scale = jnp.where(is_ctx, mod_ref[3:4, :], mod_ref[0:1, :])
        shift = jnp.where(is_ctx, mod_ref[4:5, :], mod_ref[1:2, :])
    else:
        scale, shift = mod_ref[0:1, :], mod_ref[1:2, :]
    h = _rms(x_ref[...], g_ref[...]) * (1.0 + scale) + shift
    h_ref[...] = h
    logits = jnp.dot(h, wr_ref[...], preferred_element_type=F32, precision=lax.Precision.HIGHEST)
    lane = lax.broadcasted_iota(jnp.int32, logits.shape, 1).astype(F32)
    lg = jnp.where(lane < N_EXPERTS, logits, -jnp.inf)
    m1 = jnp.max(lg, axis=-1, keepdims=True)
    i1 = jnp.min(jnp.where(lg == m1, lane, float(LANES)), axis=-1, keepdims=True)
    lg2 = jnp.where(lane == i1, -jnp.inf, lg)
    m2 = jnp.max(lg2, axis=-1, keepdims=True)
    i2 = jnp.min(jnp.where(lg2 == m2, lane, float(LANES)), axis=-1, keepdims=True)
    e = jnp.exp(m2 - m1)
    den = 1.0 + e

    member = jnp.where(jnp.logical_or(lane == i1, lane == i2), 1.0, 0.0)
    rr = lax.broadcasted_iota(jnp.int32, (tm, tm), 0)
    cc = lax.broadcasted_iota(jnp.int32, (tm, tm), 1)
    lower = jnp.where(rr > cc, 1.0, 0.0).astype(BF16)
    rank = jnp.dot(lower, member.astype(BF16), preferred_element_type=F32) + carry_ref[0:1, :]
    r1 = jnp.sum(jnp.where(lane == i1, rank, 0.0), axis=-1, keepdims=True)
    r2 = jnp.sum(jnp.where(lane == i2, rank, 0.0), axis=-1, keepdims=True)
    total = carry_ref[0:1, :] + jnp.sum(member, axis=0, keepdims=True)
    carry_ref[...] = jnp.broadcast_to(total, carry_ref.shape)
    cnt_ref[...] = jnp.broadcast_to(total, cnt_ref.shape)

    cols = (i1, i2, 1.0 / den, e / den, r1, r2)
    route = jnp.zeros_like(logits)
    for j, v in enumerate(cols):
        route = jnp.where(lane == float(j), v, route)
    route_ref[...] = route


def _router_call(xs, g, mod, wr_pad, n_rows, n_lat):
    d = xs.shape[1]
    tm = 256
    assert n_rows % tm == 0 and n_lat % tm == 0
    kern = functools.partial(_router_kernel, tm=tm, n_rows=n_rows, n_lat=n_lat)
    return pl.pallas_call(
        kern,
        grid=(n_rows // tm,),
        in_specs=[pl.BlockSpec((tm, d), lambda i: (i, 0)),
                  pl.BlockSpec((1, d), lambda i: (0, 0)),
                  pl.BlockSpec((8, d), lambda i: (0, 0)),
                  pl.BlockSpec((d, LANES), lambda i: (0, 0))],
        out_specs=[pl.BlockSpec((tm, d), lambda i: (i, 0)),
                   pl.BlockSpec((tm, LANES), lambda i: (i, 0)),
                   pl.BlockSpec((8, LANES), lambda i: (0, 0))],
        out_shape=[jax.ShapeDtypeStruct((n_rows, d), F32),
                   jax.ShapeDtypeStruct((n_rows, LANES), F32),
                   jax.ShapeDtypeStruct((8, LANES), F32)],
        scratch_shapes=[pltpu.VMEM((8, LANES), F32)],
        compiler_params=_cparams(("arbitrary",)),
        name="moe_router",
    )(xs, g.reshape(1, d), mod, wr_pad)


MOE_TB = 1024
MOE_SUB = 512
MOE_TF = 256
DMA_UNROLL = 8


def _dispatch_kernel(d1_ref, d2_ref, h_ref, xs_in_ref, xs_ref, sem, *, tm):
    del xs_in_ref

    def issue(t, carry):
        src = h_ref.at[pl.ds(t, 1), :]
        pltpu.make_async_copy(src, xs_ref.at[pl.ds(d1_ref[0, 0, t], 1), :], sem).start()
        pltpu.make_async_copy(src, xs_ref.at[pl.ds(d2_ref[0, 0, t], 1), :], sem).start()
        return carry

    lax.fori_loop(0, tm, issue, 0, unroll=DMA_UNROLL)

    def drain(t, carry):
        cp = pltpu.make_async_copy(h_ref.at[pl.ds(0, 1), :], xs_ref.at[pl.ds(0, 1), :], sem)
        cp.wait()
        cp.wait()
        return carry

    lax.fori_loop(0, tm, drain, 0, unroll=DMA_UNROLL)


def _dispatch_call(h, dest1, dest2, n_slots):
    n, d = h.shape
    tm = 256
    kern = functools.partial(_dispatch_kernel, tm=tm)
    idx = pl.BlockSpec((1, 1, tm), lambda i: (i, 0, 0), memory_space=pltpu.SMEM)
    anyspec = pl.BlockSpec(memory_space=pl.ANY)
    return pl.pallas_call(
        kern,
        grid=(n // tm,),
        in_specs=[idx, idx, pl.BlockSpec((tm, d), lambda i: (i, 0)), anyspec],
        out_specs=anyspec,
        out_shape=jax.ShapeDtypeStruct((n_slots, d), F32),
        scratch_shapes=[pltpu.SemaphoreType.DMA(())],
        input_output_aliases={3: 0},
        compiler_params=_cparams(("arbitrary",)),
        name="moe_dispatch",
    )(dest1.reshape(n // tm, 1, tm), dest2.reshape(n // tm, 1, tm), h, jnp.zeros((n_slots, d), F32))


def _grouped_ffn_kernel(be_ref, nv_ref, rv_ref, x_ref, w1_ref, w3_ref, w2_ref, o_ref, xb_ref):
    del be_ref, nv_ref
    b = pl.program_id(0)
    k = pl.program_id(1)
    rows = rv_ref[b]

    @pl.when(k == 0)
    def _():
        xb_ref[...] = x_ref[...].astype(BF16)
        o_ref[...] = jnp.zeros_like(o_ref)

    def swiglu_rows(n):
        w1 = w1_ref[0].astype(BF16)
        w3 = w3_ref[0].astype(BF16)
        w2 = w2_ref[0].astype(BF16)
        xb = xb_ref[0:n, :]
        u1 = jnp.dot(xb, w1, preferred_element_type=F32)
        u3 = jnp.dot(xb, w3, preferred_element_type=F32)
        u = (u1 * _sigmoid(u1)) * u3
        o_ref[0:n, :] += jnp.dot(u.astype(BF16), w2, preferred_element_type=F32)

    pl.when(rows > MOE_SUB)(lambda: swiglu_rows(MOE_TB))
    pl.when(jnp.logical_and(rows > 0, rows <= MOE_SUB))(lambda: swiglu_rows(MOE_SUB))


def _grouped_ffn_call(xs, w1, w3, w2, block_expert, n_valid, rows_valid):
    n_slots, d = xs.shape
    f = w1.shape[2]
    tb, tf = MOE_TB, MOE_TF
    n_k = f // tf
    assert n_slots % tb == 0 and f % tf == 0

    def blk(b, nv):
        return jnp.minimum(b, nv[0] - 1)

    def kk(b, k, nv):
        return jnp.where(b < nv[0], k, n_k - 1)

    grid_spec = pltpu.PrefetchScalarGridSpec(
        num_scalar_prefetch=3,
        grid=(n_slots // tb, n_k),
        in_specs=[pl.BlockSpec((tb, d), lambda b, k, be, nv, rv: (blk(b, nv), 0)),
                  pl.BlockSpec((1, d, tf), lambda b, k, be, nv, rv: (be[blk(b, nv)], 0, kk(b, k, nv))),
                  pl.BlockSpec((1, d, tf), lambda b, k, be, nv, rv: (be[blk(b, nv)], 0, kk(b, k, nv))),
                  pl.BlockSpec((1, tf, d), lambda b, k, be, nv, rv: (be[blk(b, nv)], kk(b, k, nv), 0))],
        out_specs=pl.BlockSpec((tb, d), lambda b, k, be, nv, rv: (b, 0)),
        scratch_shapes=[pltpu.VMEM((tb, d), BF16)])
    return pl.pallas_call(
        _grouped_ffn_kernel,
        grid_spec=grid_spec,
        out_shape=jax.ShapeDtypeStruct((n_slots, d), F32),
        compiler_params=_cparams(("arbitrary", "arbitrary")),
        name="moe_grouped_ffn",
    )(block_expert, n_valid, rows_valid, xs, w1, w3, w2)


def _combine_kernel(d1_ref, d2_ref, route_ref, x_ref, g3_ref, mod_ref, y_ref, o_ref, ya_ref, yb_ref, sem,
                    *, tm, n_rows, n_lat):
    i = pl.program_id(0)

    def issue(t, carry):
        pltpu.make_async_copy(y_ref.at[pl.ds(d1_ref[0, 0, t], 1), :], ya_ref.at[pl.ds(t, 1), :], sem).start()
        pltpu.make_async_copy(y_ref.at[pl.ds(d2_ref[0, 0, t], 1), :], yb_ref.at[pl.ds(t, 1), :], sem).start()
        return carry

    lax.fori_loop(0, tm, issue, 0, unroll=DMA_UNROLL)

    def drain(t, carry):
        cp = pltpu.make_async_copy(y_ref.at[pl.ds(0, 1), :], ya_ref.at[pl.ds(0, 1), :], sem)
        cp.wait()
        cp.wait()
        return carry

    lax.fori_loop(0, tm, drain, 0, unroll=DMA_UNROLL)

    r = route_ref[...]
    f = r[:, 2:3] * ya_ref[...] + r[:, 3:4] * yb_ref[...]
    if n_rows > n_lat:
        gate = jnp.where(i * tm >= n_lat, mod_ref[5:6, :], mod_ref[2:3, :])
    else:
        gate = mod_ref[2:3, :]
    o_ref[...] = x_ref[...] + gate * _rms(f, g3_ref[...])


def _combine_call(y, dest1, dest2, route, xs, g3, mod, n_rows, n_lat):
    d = xs.shape[1]
    tm = 256
    kern = functools.partial(_combine_kernel, tm=tm, n_rows=n_rows, n_lat=n_lat)
    idx = pl.BlockSpec((1, 1, tm), lambda i: (i, 0, 0), memory_space=pltpu.SMEM)
    return pl.pallas_call(
        kern,
        grid=(n_rows // tm,),
        in_specs=[idx, idx,
                  pl.BlockSpec((tm, LANES), lambda i: (i, 0)),
                  pl.BlockSpec((tm, d), lambda i: (i, 0)),
                  pl.BlockSpec((1, d), lambda i: (0, 0)),
                  pl.BlockSpec((8, d), lambda i: (0, 0)),
                  pl.BlockSpec(memory_space=pl.ANY)],
        out_specs=pl.BlockSpec((tm, d), lambda i: (i, 0)),
        out_shape=jax.ShapeDtypeStruct((n_rows, d), F32),
        scratch_shapes=[pltpu.VMEM((tm, d), F32), pltpu.VMEM((tm, d), F32), pltpu.SemaphoreType.DMA(())],
        compiler_params=_cparams(("arbitrary",)),
        name="moe_combine",
    )(dest1.reshape(n_rows // tm, 1, tm), dest2.reshape(n_rows // tm, 1, tm), route, xs, g3.reshape(1, d), mod, y)


def _moe_call(xs, g2, mod, wr_pad, w1, w3, w2, li, g3, n_rows, n_lat):
    h, route, cnt = _router_call(xs, g2, mod, wr_pad, n_rows, n_lat)
    tb = MOE_TB
    n_slots = -(-(2 * n_rows + N_EXPERTS * (tb - 1)) // tb) * tb
    counts = cnt[0, :N_EXPERTS].astype(jnp.int32)
    padded = (counts + tb - 1) // tb * tb
    ends = jnp.cumsum(padded)
    offs = ends - padded
    e1, e2 = route[:, 0].astype(jnp.int32), route[:, 1].astype(jnp.int32)
    dest1 = offs[e1] + route[:, 4].astype(jnp.int32)
    dest2 = offs[e2] + route[:, 5].astype(jnp.int32)
    blk_start = jnp.arange(n_slots // tb, dtype=jnp.int32) * tb
    block_expert = jnp.minimum(jnp.sum(blk_start[:, None] >= ends[None, :], axis=1), N_EXPERTS - 1)
    n_valid = (ends[-1:] // tb).astype(jnp.int32)
    rows_valid = jnp.clip((offs + counts)[block_expert] - blk_start, 0, tb).astype(jnp.int32)
    rows_valid = jnp.where(blk_start < ends[-1], rows_valid, 0)
    xg = _dispatch_call(h, dest1, dest2, n_slots)
    slab = (block_expert + li * N_EXPERTS).astype(jnp.int32)
    y = _grouped_ffn_call(xg, w1, w3, w2, slab, n_valid, rows_valid)
    return _combine_call(y, dest1, dest2, route, xs, g3, mod, n_rows, n_lat)


def _ffn_kernel(x_ref, g2_ref, mod_ref, w1_ref, w3_ref, w2_ref, g3_ref, o_ref, h_ref, acc_ref,
                *, tm, n_rows, n_lat, n_k):
    i = pl.program_id(0)
    k = pl.program_id(1)

    @pl.when(k == 0)
    def _():
        def fill(r0, r1, is_ctx):
            r = 3 if is_ctx else 0
            h = _rms(x_ref[r0:r1, :], g2_ref[...]) * (1.0 + mod_ref[r:r + 1, :]) + mod_ref[r + 1:r + 2, :]
            h_ref[r0:r1, :] = h.astype(BF16)
        _row_groups(i, tm, n_rows, n_lat, fill)
        acc_ref[...] = jnp.zeros_like(acc_ref)

    h = h_ref[...]
    u1 = jnp.dot(h, w1_ref[0], preferred_element_type=F32)
    u3 = jnp.dot(h, w3_ref[0], preferred_element_type=F32)
    u = (u1 * _sigmoid(u1)) * u3
    acc_ref[...] += jnp.dot(u.astype(BF16), w2_ref[0], preferred_element_type=F32)

    @pl.when(k == n_k - 1)
    def _():
        def fin(r0, r1, is_ctx):
            r = 5 if is_ctx else 2
            fn = _rms(acc_ref[r0:r1, :], g3_ref[...])
            o_ref[r0:r1, :] = x_ref[r0:r1, :] + mod_ref[r:r + 1, :] * fn
        _row_groups(i, tm, n_rows, n_lat, fin)


def _ffn_call(xs, g2, mod, w1, w3, w2, li, g3, n_rows, n_lat):
    d = xs.shape[1]
    f = w1.shape[2]
    tm = _pick(n_rows, (768, 512, 256))
    tf = _pick(f, (512, 256, 128))
    n_k = f // tf
    kern = functools.partial(_ffn_kernel, tm=tm, n_rows=n_rows, n_lat=n_lat, n_k=n_k)
    return pl.pallas_call(
        kern,
        grid=(n_rows // tm, n_k),
        in_specs=[pl.BlockSpec((tm, d), lambda i, k: (i, 0)),
                  pl.BlockSpec((1, d), lambda i, k: (0, 0)),
                  pl.BlockSpec((8, d), lambda i, k: (0, 0)),
                  pl.BlockSpec((1, d, tf), lambda i, k: (li, 0, k)),
                  pl.BlockSpec((1, d, tf), lambda i, k: (li, 0, k)),
                  pl.BlockSpec((1, tf, d), lambda i, k: (li, k, 0)),
                  pl.BlockSpec((1, d), lambda i, k: (0, 0))],
        out_specs=pl.BlockSpec((tm, d), lambda i, k: (i, 0)),
        out_shape=jax.ShapeDtypeStruct((n_rows, d), F32),
        scratch_shapes=[pltpu.VMEM((tm, d), BF16), pltpu.VMEM((tm, d), F32)],
        compiler_params=_cparams(("arbitrary", "arbitrary")),
        name="dense_ffn",
    )(xs, g2.reshape(1, d), mod, w1, w3, w2, g3.reshape(1, d))


def _w_in_layout_kernel(w_ref, o_ref):
    o_kr, o_dq = MLA_Q_RANK + MLA_KV_RANK, MLA_Q_RANK + MLA_KV_RANK + MLA_ROPE
    x = w_ref[0]
    n_in = x.shape[1]
    o_ref[0, :, 0:o_kr] = x[:, 0:o_kr].astype(BF16)
    o_ref[0, :, o_kr:_OFF_KR] = x[:, o_dq:n_in].astype(BF16)
    tail = jnp.concatenate([x[:, o_kr:o_dq], jnp.zeros((x.shape[0], D_IN_PAD - _OFF_KR - MLA_ROPE), F32)], axis=-1)
    o_ref[0, :, _OFF_KR:D_IN_PAD] = tail.astype(BF16)


def _prep_w_in(w_in):
    depth, d, n_in = w_in.shape
    assert n_in - MLA_ROPE == _OFF_KR
    tr = 128
    return pl.pallas_call(
        _w_in_layout_kernel,
        grid=(depth, d // tr),
        in_specs=[pl.BlockSpec((1, tr, n_in), lambda l, i: (l, i, 0))],
        out_specs=pl.BlockSpec((1, tr, D_IN_PAD), lambda l, i: (l, i, 0)),
        out_shape=jax.ShapeDtypeStruct((depth, d, D_IN_PAD), BF16),
        compiler_params=_cparams(("arbitrary", "arbitrary")),
        name="w_in_layout",
    )(w_in)


def _prep_w_uq(w_uq):
    lead = w_uq.shape[:-1]
    w = w_uq.reshape(lead + (MLA_HEADS, MLA_NOPE + MLA_ROPE))
    w = jnp.pad(w, [(0, 0)] * len(lead) + [(0, 0), (0, MLA_QK_PAD - MLA_NOPE - MLA_ROPE)])
    return w.reshape(lead + (MLA_HEADS * MLA_QK_PAD,)).astype(BF16)


def _prep_w_ukv(w_ukv):
    lead = w_ukv.shape[:-1]
    w = w_ukv.reshape(lead + (MLA_HEADS, MLA_NOPE + MLA_V))
    wk = w[..., :MLA_NOPE].reshape(lead + (MLA_HEADS * MLA_NOPE,))
    wv = w[..., MLA_NOPE:].reshape(lead + (MLA_HEADS * MLA_V,))
    return wk.astype(BF16), wv.astype(BF16)


def _rope_tables(n_lat, n_ctx):
    n_grid_rows = n_lat // GRID_W

    def table(half):
        freqs = ROPE_BASE ** (-jnp.arange(half, dtype=F32) / half)
        ar = jnp.arange(n_grid_rows, dtype=jnp.int32).astype(F32)[:, None] * freqs
        ac = jnp.arange(GRID_W, dtype=jnp.int32).astype(F32)[:, None] * freqs
        cos_r, sin_r = jnp.repeat(jnp.cos(ar), GRID_W, axis=0), jnp.repeat(jnp.sin(ar), GRID_W, axis=0)
        cos_c, sin_c = jnp.tile(jnp.cos(ac), (n_grid_rows, 1)), jnp.tile(jnp.sin(ac), (n_grid_rows, 1))
        z = jnp.zeros((n_lat, half), F32)
        pad = jnp.zeros((n_lat, LANES - 4 * half), F32)
        cos = jnp.concatenate([cos_r, cos_r, cos_c, cos_c, 1.0 + pad], axis=-1)
        sin_lo = jnp.concatenate([-sin_r, z, -sin_c, z, pad], axis=-1)
        sin_hi = jnp.concatenate([z, sin_r, z, sin_c, pad], axis=-1)
        ident = jnp.ones((n_ctx, LANES), F32), jnp.zeros((n_ctx, LANES), F32), jnp.zeros((n_ctx, LANES), F32)
        return tuple(jnp.concatenate([a, b], axis=0) for a, b in zip((cos, sin_lo, sin_hi), ident))

    return table(DIFF_QK // 4) + table(MLA_ROPE // 4)


def kernel(x, c, ctx, c_ctx, w_mod, b_mod, g_norm, w_in, w_uq, g_qn, w_ukv, g_kvn, lam, g_sub, w_po_mla,
           w_po_diff, w_out, w1_dense, w3_dense, w2_dense, w_router, w1_moe, w3_moe, w2_moe):
    b, s, d = x.shape
    n_ctx = ctx.shape[1]
    depth = w_mod.shape[0]
    assert b == 1 and d == D_MODEL and s % GRID_W == 0
    n_tok = s + n_ctx

    xs = jnp.concatenate([x[0], ctx[0]], axis=0)
    c8 = jnp.concatenate([c, c_ctx[None, :], jnp.zeros((6, d), F32)], axis=0)
    mods = _mod_call(c8, w_mod, b_mod)

    w_in_p = _prep_w_in(w_in)
    w_uq_p = _prep_w_uq(w_uq)
    w_k_p, w_v_p = _prep_w_ukv(w_ukv)
    w_pm, w_pd, w_o = w_po_mla.astype(BF16), w_po_diff.astype(BF16), w_out.astype(BF16)
    w1d, w3d, w2d = w1_dense.astype(BF16), w3_dense.astype(BF16), w2_dense.astype(BF16)
    wr_pad = jnp.pad(w_router, ((0, 0), (0, 0), (0, LANES - N_EXPERTS)))
    flat = lambda w: w.reshape((-1,) + w.shape[2:])
    w1m, w3m, w2m = flat(w1_moe), flat(w3_moe), flat(w2_moe)
    tabs = _rope_tables(s, n_ctx)
    zrow = jnp.zeros((d,), F32)

    for layer in range(depth):
        need_ctx = layer < depth - 1
        lat, cx = mods[layer, 0], mods[layer, 1]
        seg = lambda v, k: v[k * d:(k + 1) * d]
        gn = g_norm[layer]

        mod1 = jnp.stack([seg(lat, 1), seg(lat, 0), seg(cx, 1), seg(cx, 0), zrow, zrow, zrow, zrow])
        proj = _norm_matmul_call(xs, gn[0], mod1, w_in_p, layer, s)
        qm, km, vm, qd, kd = _prep_call(proj, g_qn[layer].reshape(1, -1), g_kvn[layer].reshape(1, -1),
                                        w_uq_p, w_k_p, w_v_p, layer, tabs)
        lam_init = 0.8 - 0.6 * math.exp(-0.3 * layer)
        o_mla = _mla_attn_call(qm, km, vm, 0, s, 0, n_tok)
        o_d = _diff_attn_call(qd, kd, proj, lam[layer], g_sub[layer], lam_init, 0, s, 0, n_tok)
        o_mla_c = o_d_c = None
        if need_ctx:
            o_mla_c = _mla_attn_call(qm, km, vm, s, n_ctx, s, n_ctx)
            o_d_c = _diff_attn_call(qd, kd, proj, lam[layer], g_sub[layer], lam_init, s, n_ctx, s, n_ctx)
        n_rows = n_tok if need_ctx else s
        gate1 = jnp.stack([seg(lat, 2), seg(cx, 2), zrow, zrow, zrow, zrow, zrow, zrow])
        xs = _post_call(o_mla, o_d, o_mla_c, o_d_c, proj, xs, w_pm, w_pd, w_o, layer, gn[1], gate1, s)

        mod2 = jnp.stack([seg(lat, 4), seg(lat, 3), seg(lat, 5), seg(cx, 4), seg(cx, 3), seg(cx, 5), zrow, zrow])
        i = layer // 2
        if layer % 2 == 0:
            xs = _ffn_call(xs, gn[2], mod2, w1d, w3d, w2d, i, gn[3], n_rows, s)
        else:
            xs = _moe_call(xs, gn[2], mod2, wr_pad[i], w1m, w3m, w2m, i, gn[3], n_rows, s)
    return xs[:s][None]
```

```python
import functools
import math

import jax
import jax.numpy as jnp
from jax import lax
from jax.experimental import pallas as pl
from jax.experimental.pallas import tpu as pltpu

F32 = jnp.float32
BF16 = jnp.bfloat16

D_MODEL = 2048
GRID_W = 64
MLA_HEADS = 8
MLA_Q_RANK = 512
MLA_KV_RANK = 512
MLA_NOPE = 128
MLA_ROPE = 64
MLA_V = 128
MLA_QK_PAD = 256
DIFF_HEADS = 4
DIFF_QK = 128
DIFF_V = 256
N_EXPERTS = 8
ROPE_BASE = 10000.0
NORM_EPS = 1e-6
LANES = 128
LOG2E = math.log2(math.e)

_OFF_CQ, _OFF_CKV, _OFF_DQ, _OFF_DK, _OFF_DV, _OFF_GA, _OFF_GB, _OFF_KR = (
    0, 512, 1024, 2048, 3072, 4096, 6144, 8192)
D_IN_PAD = 8448

VMEM_LIMIT = 56 * 1024 * 1024


def _pick(n, cands):
    for c in cands:
        if n % c == 0:
            return c
    raise ValueError(f"no tile for {n} in {cands}")


def _cparams(sem):
    return pltpu.CompilerParams(dimension_semantics=sem, vmem_limit_bytes=VMEM_LIMIT)


def _rms(xf, g):
    ms = jnp.mean(xf * xf, axis=-1, keepdims=True)
    return xf * lax.rsqrt(ms + NORM_EPS) * g


def _sigmoid(v):
    return 1.0 / (1.0 + jnp.exp(-v))


def _row_groups(i, tm, n_rows, n_lat, fn):
    if n_rows <= n_lat:
        fn(0, tm, False)
        return
    nb = n_rows // tm
    b, off = divmod(n_lat, tm)
    if b > 0:
        pl.when(i < b)(lambda: fn(0, tm, False))
    if off:
        def _split():
            fn(0, off, False)
            fn(off, tm, True)
        pl.when(i == b)(_split)
        if nb > b + 1:
            pl.when(i > b)(lambda: fn(0, tm, True))
    else:
        pl.when(i >= b)(lambda: fn(0, tm, True))


def _mod_kernel(c_ref, w_ref, b_ref, o_ref):
    cv = c_ref[...]
    s = cv * _sigmoid(cv)
    o_ref[0] = jnp.dot(s, w_ref[0], preferred_element_type=F32,
                       precision=lax.Precision.HIGHEST) + b_ref[0]


def _mod_call(c8, w_mod, b_mod):
    depth, d, n6 = w_mod.shape
    tn = _pick(n6, (1536, 1024, 512, 256, 128))
    return pl.pallas_call(
        _mod_kernel,
        grid=(depth, n6 // tn),
        in_specs=[pl.BlockSpec((8, d), lambda l, j: (0, 0)),
                  pl.BlockSpec((1, d, tn), lambda l, j: (l, 0, j)),
                  pl.BlockSpec((1, 1, tn), lambda l, j: (l, 0, j))],
        out_specs=pl.BlockSpec((1, 8, tn), lambda l, j: (l, 0, j)),
        out_shape=jax.ShapeDtypeStruct((depth, 8, n6), F32),
        compiler_params=_cparams(("arbitrary", "arbitrary")),
        name="adaln_mod",
    )(c8, w_mod, b_mod.reshape(depth, 1, n6))


def _norm_matmul_kernel(x_ref, g_ref, mod_ref, w_ref, o_ref, h_ref, *, tm, n_rows, n_lat):
    i = pl.program_id(0)
    j = pl.program_id(1)

    @pl.when(j == 0)
    def _():
        def fill(r0, r1, is_ctx):
            r = 2 if is_ctx else 0
            h = _rms(x_ref[r0:r1, :], g_ref[...]) * (1.0 + mod_ref[r:r + 1, :]) + mod_ref[r + 1:r + 2, :]
            h_ref[r0:r1, :] = h.astype(BF16)
        _row_groups(i, tm, n_rows, n_lat, fill)

    o_ref[...] = lax.dot_general(h_ref[...], w_ref[0], (((1,), (1,)), ((), ())),
                                 preferred_element_type=F32).astype(o_ref.dtype)


def _norm_matmul_call(xs, g, mod, w, layer, n_lat):
    n_rows, d = xs.shape
    n_out = w.shape[1]
    tm = _pick(n_rows, (768, 512, 256))
    tn = _pick(n_out, (1408, 1024, 768, 512, 256))
    kern = functools.partial(_norm_matmul_kernel, tm=tm, n_rows=n_rows, n_lat=n_lat)
    return pl.pallas_call(
        kern,
        grid=(n_rows // tm, n_out // tn),
        in_specs=[pl.BlockSpec((tm, d), lambda i, j: (i, 0)),
                  pl.BlockSpec((1, d), lambda i, j: (0, 0)),
                  pl.BlockSpec((8, d), lambda i, j: (0, 0)),
                  pl.BlockSpec((1, tn, d), lambda i, j: (layer, j, 0))],
        out_specs=pl.BlockSpec((tm, tn), lambda i, j: (i, j)),
        out_shape=jax.ShapeDtypeStruct((n_rows, n_out), BF16),
        scratch_shapes=[pltpu.VMEM((tm, d), BF16)],
        compiler_params=_cparams(("arbitrary", "arbitrary")),
        name="norm_in_proj",
    )(xs, g.reshape(1, d), mod, w)


def _rope(xf, tab, half):
    cos, sin_lo, sin_hi = tab
    return xf * cos + pltpu.roll(xf, LANES - half, 1) * sin_lo + pltpu.roll(xf, half, 1) * sin_hi


def _prep_kernel(cq_ref, ckv_ref, dq_ref, dk_ref, kr_ref, gq_ref, gkv_ref, wuq_ref, wk_ref, wv_ref,
                 cosd_ref, sld_ref, shd_ref, cosm_ref, slm_ref, shm_ref,
                 qm_ref, km_ref, vm_ref, qd_ref, kd_ref, *, s_mla, s_diff):
    tabm = (cosm_ref[...], slm_ref[...], shm_ref[...])
    tabd = (cosd_ref[...], sld_ref[...], shd_ref[...])
    half_m, half_d = MLA_ROPE // 4, DIFF_QK // 4

    cqn = _rms(cq_ref[...].astype(F32), gq_ref[...]).astype(BF16)
    q = jnp.dot(cqn, wuq_ref[0], preferred_element_type=F32)
    for h in range(MLA_HEADS):
        base = h * MLA_QK_PAD
        qm_ref[h, :, 0:LANES] = (q[:, base:base + LANES] * s_mla).astype(BF16)
        qr = _rope(q[:, base + LANES:base + 2 * LANES], tabm, half_m)
        qm_ref[h, :, LANES:2 * LANES] = (qr * s_mla).astype(BF16)

    ckvn = _rms(ckv_ref[...].astype(F32), gkv_ref[...]).astype(BF16)
    kn = jnp.dot(ckvn, wk_ref[0], preferred_element_type=F32)
    vv = jnp.dot(ckvn, wv_ref[0], preferred_element_type=F32)
    kr = _rope(kr_ref[...].astype(F32), tabm, half_m).astype(BF16)
    for h in range(MLA_HEADS):
        km_ref[h, :, 0:LANES] = kn[:, h * LANES:(h + 1) * LANES].astype(BF16)
        km_ref[h, :, LANES:2 * LANES] = kr
        vm_ref[h] = vv[:, h * LANES:(h + 1) * LANES].astype(BF16)

    for hc in range(2 * DIFF_HEADS):
        xq = dq_ref[:, hc * LANES:(hc + 1) * LANES].astype(F32)
        qd_ref[hc] = (_rope(xq, tabd, half_d) * s_diff).astype(BF16)
        xk = dk_ref[:, hc * LANES:(hc + 1) * LANES].astype(F32)
        kd_ref[hc] = _rope(xk, tabd, half_d).astype(BF16)


def _prep_call(proj, gq, gkv, wuq, wk, wv, layer, tabs):
    n = proj.shape[0]
    tm = _pick(n, (384, 256, 128))
    nh, nd = MLA_HEADS, 2 * DIFF_HEADS
    kern = functools.partial(_prep_kernel, s_mla=float((MLA_NOPE + MLA_ROPE) ** -0.5 * LOG2E),
                             s_diff=float(DIFF_QK ** -0.5 * LOG2E))
    row = lambda w, c: pl.BlockSpec((tm, w), lambda i, c=c: (i, c))
    const = lambda a: pl.BlockSpec(a.shape, lambda i: (0,) * a.ndim)
    wspec = lambda a: pl.BlockSpec((1,) + a.shape[1:], lambda i: (layer, 0, 0))
    tab = pl.BlockSpec((tm, LANES), lambda i: (i, 0))
    return pl.pallas_call(
        kern,
        grid=(n // tm,),
        in_specs=[row(512, _OFF_CQ // 512), row(512, _OFF_CKV // 512), row(1024, _OFF_DQ // 1024),
                  row(1024, _OFF_DK // 1024), row(LANES, _OFF_KR // LANES),
                  const(gq), const(gkv), wspec(wuq), wspec(wk), wspec(wv)] + [tab] * 6,
        out_specs=[pl.BlockSpec((nh, tm, MLA_QK_PAD), lambda i: (0, i, 0)),
                   pl.BlockSpec((nh, tm, MLA_QK_PAD), lambda i: (0, i, 0)),
                   pl.BlockSpec((nh, tm, MLA_V), lambda i: (0, i, 0)),
                   pl.BlockSpec((nd, tm, DIFF_QK), lambda i: (0, i, 0)),
                   pl.BlockSpec((nd, tm, DIFF_QK), lambda i: (0, i, 0))],
        out_shape=[jax.ShapeDtypeStruct((nh, n, MLA_QK_PAD), BF16),
                   jax.ShapeDtypeStruct((nh, n, MLA_QK_PAD), BF16),
                   jax.ShapeDtypeStruct((nh, n, MLA_V), BF16),
                   jax.ShapeDtypeStruct((nd, n, DIFF_QK), BF16),
                   jax.ShapeDtypeStruct((nd, n, DIFF_QK), BF16)],
        compiler_params=_cparams(("arbitrary",)),
        name="qkv_prep",
    )(proj, proj, proj, proj, proj, gq, gkv, wuq, wk, wv, *tabs)


def _softmax_pv(q, k_at, v_at, tk, n_chunks, dv):
    tq = q.shape[0]
    m = jnp.full((tq, 1), -jnp.inf, F32)
    l = jnp.zeros((tq, 1), F32)
    acc = jnp.zeros((tq, dv), F32)
    for c in range(n_chunks):
        off = c * tk
        s = lax.dot_general(q, k_at(off), (((1,), (1,)), ((), ())), preferred_element_type=F32)
        m_new = jnp.maximum(m, jnp.max(s, axis=-1, keepdims=True))
        alpha = jnp.exp2(m - m_new)
        p = jnp.exp2(s - m_new)
        l = alpha * l + jnp.sum(p, axis=-1, keepdims=True)
        acc = alpha * acc + jnp.dot(p.astype(BF16), v_at(off), preferred_element_type=F32)
        m = m_new
    return acc / l


def _mla_attn_kernel(q_ref, k_ref, v_ref, o_ref, *, tk, n_chunks, n_sub):
    ts = q_ref.shape[1] // n_sub
    qs = [q_ref[0, i * ts:(i + 1) * ts, :] for i in range(n_sub)]
    m = [jnp.full((ts, 1), -jnp.inf, F32) for _ in range(n_sub)]
    l = [jnp.zeros((ts, 1), F32) for _ in range(n_sub)]
    acc = [jnp.zeros((ts, MLA_V), F32) for _ in range(n_sub)]
    for c in range(n_chunks):
        k = k_ref[0, c * tk:(c + 1) * tk, :]
        v = v_ref[0, c * tk:(c + 1) * tk, :]
        for i in range(n_sub):
            s = lax.dot_general(qs[i], k, (((1,), (1,)), ((), ())), preferred_element_type=F32)
            m_new = jnp.maximum(m[i], jnp.max(s, axis=-1, keepdims=True))
            alpha = jnp.exp2(m[i] - m_new)
            p = jnp.exp2(s - m_new)
            l[i] = alpha * l[i] + jnp.sum(p, axis=-1, keepdims=True)
            acc[i] = alpha * acc[i] + jnp.dot(p.astype(BF16), v, preferred_element_type=F32)
            m[i] = m_new
    for i in range(n_sub):
        o_ref[i * ts:(i + 1) * ts, :] = (acc[i] / l[i]).astype(o_ref.dtype)


def _mla_attn_call(qm, km, vm, q_row0, n_q, kv_row0, n_kv):
    tq = _pick(n_q, (1024, 512, 256, 128))
    tk = _pick(n_kv, (2816, 768, 512, 256, 128))
    assert q_row0 % tq == 0 and kv_row0 % n_kv == 0
    qb, kb = q_row0 // tq, kv_row0 // n_kv
    kern = functools.partial(_mla_attn_kernel, tk=tk, n_chunks=n_kv // tk, n_sub=max(1, tq // 512))
    return pl.pallas_call(
        kern,
        grid=(MLA_HEADS, n_q // tq),
        in_specs=[pl.BlockSpec((1, tq, MLA_QK_PAD), lambda h, i: (h, i + qb, 0)),
                  pl.BlockSpec((1, n_kv, MLA_QK_PAD), lambda h, i: (h, kb, 0)),
                  pl.BlockSpec((1, n_kv, MLA_V), lambda h, i: (h, kb, 0))],
        out_specs=pl.BlockSpec((tq, MLA_V), lambda h, i: (i, h)),
        out_shape=jax.ShapeDtypeStruct((n_q, MLA_HEADS * MLA_V), BF16),
        compiler_params=_cparams(("arbitrary", "arbitrary")),
        name="mla_attention",
    )(qm, km, vm)


def _diff_attn_kernel(q_ref, k_ref, v_ref, lam_ref, gs_ref, o_ref, *, tk, n_chunks):
    lam_init = lam_ref[4:5, 0:1]
    outs = []
    for comp in range(2):
        outs.append(_softmax_pv(q_ref[comp],
                                lambda off, comp=comp: k_ref[comp, pl.ds(off, tk), :],
                                lambda off: v_ref[pl.ds(off, tk), :],
                                tk, n_chunks, DIFF_V))
    lf = lam_ref[...]
    lam_full = (jnp.exp(jnp.sum(lf[0:1] * lf[1:2], axis=-1, keepdims=True))
                - jnp.exp(jnp.sum(lf[2:3] * lf[3:4], axis=-1, keepdims=True)) + lam_init)
    d = outs[0] - lam_full * outs[1]
    o_ref[...] = (_rms(d, gs_ref[...]) * (1.0 - lam_init)).astype(o_ref.dtype)


def _diff_attn_call(qd, kd, proj, lam, gsub, lam_init, q_row0, n_q, kv_row0, n_kv):
    tq = _pick(n_q, (512, 256, 128))
    tk = _pick(n_kv, (2816, 768, 512, 256, 128))
    assert q_row0 % tq == 0 and kv_row0 % n_kv == 0
    qb, kb = q_row0 // tq, kv_row0 // n_kv
    vcol = _OFF_DV // DIFF_V
    kern = functools.partial(_diff_attn_kernel, tk=tk, n_chunks=n_kv // tk)
    in_specs = [pl.BlockSpec((2, tq, DIFF_QK), lambda h, i: (h, i + qb, 0)),
                pl.BlockSpec((2, n_kv, DIFF_QK), lambda h, i: (h, kb, 0)),
                pl.BlockSpec((n_kv, DIFF_V), lambda h, i: (kb, vcol + h)),
                pl.BlockSpec((8, DIFF_QK), lambda h, i: (0, 0)),
                pl.BlockSpec((1, DIFF_V), lambda h, i: (0, 0))]
    lam8 = jnp.concatenate([lam, jnp.full((4, DIFF_QK), lam_init, F32)], axis=0)
    return pl.pallas_call(
        kern,
        grid=(DIFF_HEADS, n_q // tq),
        in_specs=in_specs,
        out_specs=pl.BlockSpec((tq, DIFF_V), lambda h, i: (i, h)),
        out_shape=jax.ShapeDtypeStruct((n_q, DIFF_HEADS * DIFF_V), BF16),
        compiler_params=_cparams(("arbitrary", "arbitrary")),
        name="diff_attention",
    )(qd, kd, proj, lam8, gsub.reshape(1, DIFF_V))


def _post_kernel(*refs, tm, n_lat, with_ctx):
    if with_ctx:
        (om_ref, od_ref, omc_ref, odc_ref, ga_ref, gb_ref, x_ref, wpm_ref, wpd_ref, wo_ref, g_ref, gate_ref,
         o_ref) = refs
    else:
        om_ref, od_ref, ga_ref, gb_ref, x_ref, wpm_ref, wpd_ref, wo_ref, g_ref, gate_ref, o_ref = refs
    i = pl.program_id(0)
    om, od, gate = om_ref[...], od_ref[...], gate_ref[0:1, :]
    if with_ctx:
        is_ctx = i * tm >= n_lat
        om = jnp.where(is_ctx, omc_ref[...], om)
        od = jnp.where(is_ctx, odc_ref[...], od)
        gate = jnp.where(is_ctx, gate_ref[1:2, :], gate)
    a = jnp.dot(om, wpm_ref[0], preferred_element_type=F32)
    b = jnp.dot(od, wpd_ref[0], preferred_element_type=F32)
    y = _sigmoid(ga_ref[...].astype(F32)) * a + _sigmoid(gb_ref[...].astype(F32)) * b
    z = jnp.dot(y.astype(BF16), wo_ref[0], preferred_element_type=F32)
    o_ref[...] = x_ref[...] + gate * _rms(z, g_ref[...])


def _post_call(o_mla, o_d, o_mla_ctx, o_d_ctx, proj, xs, wpm, wpd, wo, layer, g, gate, n_lat):
    d = xs.shape[1]
    tm = 256
    with_ctx = o_mla_ctx is not None
    n_rows = n_lat + (o_mla_ctx.shape[0] if with_ctx else 0)
    assert n_rows % tm == 0 and n_lat % tm == 0
    nb_lat = n_lat // tm
    kern = functools.partial(_post_kernel, tm=tm, n_lat=n_lat, with_ctx=with_ctx)
    const = lambda a: pl.BlockSpec((1,) + a.shape[1:], lambda i: (layer, 0, 0), pipeline_mode=pl.Buffered(1))
    lat = lambda a: pl.BlockSpec((tm, a.shape[1]), lambda i: (jnp.minimum(i, nb_lat - 1), 0))
    cxs = lambda a: pl.BlockSpec((tm, a.shape[1]), lambda i: (jnp.maximum(i - nb_lat, 0), 0))
    in_specs = [lat(o_mla), lat(o_d)]
    args = [o_mla, o_d]
    if with_ctx:
        in_specs += [cxs(o_mla_ctx), cxs(o_d_ctx)]
        args += [o_mla_ctx, o_d_ctx]
    in_specs += [pl.BlockSpec((tm, d), lambda i: (i, _OFF_GA // D_MODEL)),
                 pl.BlockSpec((tm, d), lambda i: (i, _OFF_GB // D_MODEL)),
                 pl.BlockSpec((tm, d), lambda i: (i, 0)),
                 const(wpm), const(wpd), const(wo),
                 pl.BlockSpec((1, d), lambda i: (0, 0)),
                 pl.BlockSpec((8, d), lambda i: (0, 0))]
    args += [proj, proj, xs, wpm, wpd, wo, g.reshape(1, d), gate]
    return pl.pallas_call(
        kern,
        grid=(n_rows // tm,),
        in_specs=in_specs,
        out_specs=pl.BlockSpec((tm, d), lambda i: (i, 0)),
        out_shape=jax.ShapeDtypeStruct((n_rows, d), F32),
        compiler_params=_cparams(("arbitrary",)),
        name="merge_out_proj",
    )(*args)


def _router_kernel(x_ref, g_ref, mod_ref, wr_ref, h_ref, route_ref, cnt_ref, carry_ref, *, tm, n_rows, n_lat):
    i = pl.program_id(0)

    @pl.when(i == 0)
    def _():
        carry_ref[...] = jnp.zeros_like(carry_ref)

    if n_rows > n_lat:
        is_ctx = i * tm >= n_lat
        scale = jnp.where(is_ctx, mod_ref[3:4, :], mod_ref[0:1, :])
        shift = jnp.where(is_ctx, mod_ref[4:5, :], mod_ref[1:2, :])
    else:
        scale, shift = mod_ref[0:1, :], mod_ref[1:2, :]
    h = _rms(x_ref[...], g_ref[...]) * (1.0 + scale) + shift
    h_ref[...] = h
    logits = jnp.dot(h, wr_ref[...], preferred_element_type=F32, precision=lax.Precision.HIGHEST)
    lane = lax.broadcasted_iota(jnp.int32, logits.shape, 1).astype(F32)
    lg = jnp.where(lane < N_EXPERTS, logits, -jnp.inf)
    m1 = jnp.max(lg, axis=-1, keepdims=True)
    i1 = jnp.min(jnp.where(lg == m1, lane, float(LANES)), axis=-1, keepdims=True)
    lg2 = jnp.where(lane == i1, -jnp.inf, lg)
    m2 = jnp.max(lg2, axis=-1, keepdims=True)
    i2 = jnp.min(jnp.where(lg2 == m2, lane, float(LANES)), axis=-1, keepdims=True)
    e = jnp.exp(m2 - m1)
    den = 1.0 + e

    member = jnp.where(jnp.logical_or(lane == i1, lane == i2), 1.0, 0.0)
    rr = lax.broadcasted_iota(jnp.int32, (tm, tm), 0)
    cc = lax.broadcasted_iota(jnp.int32, (tm, tm), 1)
    lower = jnp.where(rr > cc, 1.0, 0.0).astype(BF16)
    rank = jnp.dot(lower, member.astype(BF16), preferred_element_type=F32) + carry_ref[0:1, :]
    r1 = jnp.sum(jnp.where(lane == i1, rank, 0.0), axis=-1, keepdims=True)
    r2 = jnp.sum(jnp.where(lane == i2, rank, 0.0), axis=-1, keepdims=True)
    total = carry_ref[0:1, :] + jnp.sum(member, axis=0, keepdims=True)
    carry_ref[...] = jnp.broadcast_to(total, carry_ref.shape)
    cnt_ref[...] = jnp.broadcast_to(total, cnt_ref.shape)

    cols = (i1, i2, 1.0 / den, e / den, r1, r2)
    route = jnp.zeros_like(logits)
    for j, v in enumerate(cols):
        route = jnp.where(lane == float(j), v, route)
    route_ref[...] = route


def _router_call(xs, g, mod, wr_pad, n_rows, n_lat):
    d = xs.shape[1]
    tm = 256
    assert n_rows % tm == 0 and n_lat % tm == 0
    kern = functools.partial(_router_kernel, tm=tm, n_rows=n_rows, n_lat=n_lat)
    return pl.pallas_call(
        kern,
        grid=(n_rows // tm,),
        in_specs=[pl.BlockSpec((tm, d), lambda i: (i, 0)),
                  pl.BlockSpec((1, d), lambda i: (0, 0)),
                  pl.BlockSpec((8, d), lambda i: (0, 0)),
                  pl.BlockSpec((d, LANES), lambda i: (0, 0))],
        out_specs=[pl.BlockSpec((tm, d), lambda i: (i, 0)),
                   pl.BlockSpec((tm, LANES), lambda i: (i, 0)),
                   pl.BlockSpec((8, LANES), lambda i: (0, 0))],
        out_shape=[jax.ShapeDtypeStruct((n_rows, d), F32),
                   jax.ShapeDtypeStruct((n_rows, LANES), F32),
                   jax.ShapeDtypeStruct((8, LANES), F32)],
        scratch_shapes=[pltpu.VMEM((8, LANES), F32)],
        compiler_params=_cparams(("arbitrary",)),
        name="moe_router",
    )(xs, g.reshape(1, d), mod, wr_pad)


MOE_TB = 1024
MOE_SUB = 512
MOE_TF = 256
DMA_UNROLL = 8


def _dispatch_kernel(d1_ref, d2_ref, h_ref, xs_in_ref, xs_ref, sem, *, tm):
    del xs_in_ref

    def issue(t, carry):
        src = h_ref.at[pl.ds(t, 1), :]
        pltpu.make_async_copy(src, xs_ref.at[pl.ds(d1_ref[0, 0, t], 1), :], sem).start()
        pltpu.make_async_copy(src, xs_ref.at[pl.ds(d2_ref[0, 0, t], 1), :], sem).start()
        return carry

    lax.fori_loop(0, tm, issue, 0, unroll=DMA_UNROLL)

    def drain(t, carry):
        cp = pltpu.make_async_copy(h_ref.at[pl.ds(0, 1), :], xs_ref.at[pl.ds(0, 1), :], sem)
        cp.wait()
        cp.wait()
        return carry

    lax.fori_loop(0, tm, drain, 0, unroll=DMA_UNROLL)


def _dispatch_call(h, dest1, dest2, n_slots):
    n, d = h.shape
    tm = 256
    kern = functools.partial(_dispatch_kernel, tm=tm)
    idx = pl.BlockSpec((1, 1, tm), lambda i: (i, 0, 0), memory_space=pltpu.SMEM)
    anyspec = pl.BlockSpec(memory_space=pl.ANY)
    return pl.pallas_call(
        kern,
        grid=(n // tm,),
        in_specs=[idx, idx, pl.BlockSpec((tm, d), lambda i: (i, 0)), anyspec],
        out_specs=anyspec,
        out_shape=jax.ShapeDtypeStruct((n_slots, d), F32),
        scratch_shapes=[pltpu.SemaphoreType.DMA(())],
        input_output_aliases={3: 0},
        compiler_params=_cparams(("arbitrary",)),
        name="moe_dispatch",
    )(dest1.reshape(n // tm, 1, tm), dest2.reshape(n // tm, 1, tm), h, jnp.zeros((n_slots, d), F32))


def _grouped_ffn_kernel(be_ref, nv_ref, rv_ref, x_ref, w1_ref, w3_ref, w2_ref, o_ref, xb_ref):
    del be_ref, nv_ref
    b = pl.program_id(0)
    k = pl.program_id(1)
    rows = rv_ref[b]

    @pl.when(k == 0)
    def _():
        xb_ref[...] = x_ref[...].astype(BF16)
        o_ref[...] = jnp.zeros_like(o_ref)

    def swiglu_rows(n):
        w1 = w1_ref[0].astype(BF16)
        w3 = w3_ref[0].astype(BF16)
        w2 = w2_ref[0].astype(BF16)
        xb = xb_ref[0:n, :]
        u1 = jnp.dot(xb, w1, preferred_element_type=F32)
        u3 = jnp.dot(xb, w3, preferred_element_type=F32)
        u = (u1 * _sigmoid(u1)) * u3
        o_ref[0:n, :] += jnp.dot(u.astype(BF16), w2, preferred_element_type=F32)

    pl.when(rows > MOE_SUB)(lambda: swiglu_rows(MOE_TB))
    pl.when(jnp.logical_and(rows > 0, rows <= MOE_SUB))(lambda: swiglu_rows(MOE_SUB))


def _grouped_ffn_call(xs, w1, w3, w2, block_expert, n_valid, rows_valid):
    n_slots, d = xs.shape
    f = w1.shape[2]
    tb, tf = MOE_TB, MOE_TF
    n_k = f // tf
    assert n_slots % tb == 0 and f % tf == 0

    def blk(b, nv):
        return jnp.minimum(b, nv[0] - 1)

    def kk(b, k, nv):
        return jnp.where(b < nv[0], k, n_k - 1)

    grid_spec = pltpu.PrefetchScalarGridSpec(
        num_scalar_prefetch=3,
        grid=(n_slots // tb, n_k),
        in_specs=[pl.BlockSpec((tb, d), lambda b, k, be, nv, rv: (blk(b, nv), 0)),
                  pl.BlockSpec((1, d, tf), lambda b, k, be, nv, rv: (be[blk(b, nv)], 0, kk(b, k, nv))),
                  pl.BlockSpec((1, d, tf), lambda b, k, be, nv, rv: (be[blk(b, nv)], 0, kk(b, k, nv))),
                  pl.BlockSpec((1, tf, d), lambda b, k, be, nv, rv: (be[blk(b, nv)], kk(b, k, nv), 0))],
        out_specs=pl.BlockSpec((tb, d), lambda b, k, be, nv, rv: (b, 0)),
        scratch_shapes=[pltpu.VMEM((tb, d), BF16)])
    return pl.pallas_call(
        _grouped_ffn_kernel,
        grid_spec=grid_spec,
        out_shape=jax.ShapeDtypeStruct((n_slots, d), F32),
        compiler_params=_cparams(("arbitrary", "arbitrary")),
        name="moe_grouped_ffn",
    )(block_expert, n_valid, rows_valid, xs, w1, w3, w2)


def _combine_kernel(d1_ref, d2_ref, route_ref, x_ref, g3_ref, mod_ref, y_ref, o_ref, ya_ref, yb_ref, sem,
                    *, tm, n_rows, n_lat):
    i = pl.program_id(0)

    def issue(t, carry):
        pltpu.make_async_copy(y_ref.at[pl.ds(d1_ref[0, 0, t], 1), :], ya_ref.at[pl.ds(t, 1), :], sem).start()
        pltpu.make_async_copy(y_ref.at[pl.ds(d2_ref[0, 0, t], 1), :], yb_ref.at[pl.ds(t, 1), :], sem).start()
        return carry

    lax.fori_loop(0, tm, issue, 0, unroll=DMA_UNROLL)

    def drain(t, carry):
        cp = pltpu.make_async_copy(y_ref.at[pl.ds(0, 1), :], ya_ref.at[pl.ds(0, 1), :], sem)
        cp.wait()
        cp.wait()
        return carry

    lax.fori_loop(0, tm, drain, 0, unroll=DMA_UNROLL)

    r = route_ref[...]
    f = r[:, 2:3] * ya_ref[...] + r[:, 3:4] * yb_ref[...]
    if n_rows > n_lat:
        gate = jnp.where(i * tm >= n_lat, mod_ref[5:6, :], mod_ref[2:3, :])
    else:
        gate = mod_ref[2:3, :]
    o_ref[...] = x_ref[...] + gate * _rms(f, g3_ref[...])


def _combine_call(y, dest1, dest2, route, xs, g3, mod, n_rows, n_lat):
    d = xs.shape[1]
    tm = 256
    kern = functools.partial(_combine_kernel, tm=tm, n_rows=n_rows, n_lat=n_lat)
    idx = pl.BlockSpec((1, 1, tm), lambda i: (i, 0, 0), memory_space=pltpu.SMEM)
    return pl.pallas_call(
        kern,
        grid=(n_rows // tm,),
        in_specs=[idx, idx,
                  pl.BlockSpec((tm, LANES), lambda i: (i, 0)),
                  pl.BlockSpec((tm, d), lambda i: (i, 0)),
                  pl.BlockSpec((1, d), lambda i: (0, 0)),
                  pl.BlockSpec((8, d), lambda i: (0, 0)),
                  pl.BlockSpec(memory_space=pl.ANY)],
        out_specs=pl.BlockSpec((tm, d), lambda i: (i, 0)),
        out_shape=jax.ShapeDtypeStruct((n_rows, d), F32),
        scratch_shapes=[pltpu.VMEM((tm, d), F32), pltpu.VMEM((tm, d), F32), pltpu.SemaphoreType.DMA(())],
        compiler_params=_cparams(("arbitrary",)),
        name="moe_combine",
    )(dest1.reshape(n_rows // tm, 1, tm), dest2.reshape(n_rows // tm, 1, tm), route, xs, g3.reshape(1, d), mod, y)


def _moe_call(xs, g2, mod, wr_pad, w1, w3, w2, li, g3, n_rows, n_lat):
    h, route, cnt = _router_call(xs, g2, mod, wr_pad, n_rows, n_lat)
    tb = MOE_TB
    n_slots = -(-(2 * n_rows + N_EXPERTS * (tb - 1)) // tb) * tb
    counts = cnt[0, :N_EXPERTS].astype(jnp.int32)
    padded = (counts + tb - 1) // tb * tb
    ends = jnp.cumsum(padded)
    offs = ends - padded
    e1, e2 = route[:, 0].astype(jnp.int32), route[:, 1].astype(jnp.int32)
    dest1 = offs[e1] + route[:, 4].astype(jnp.int32)
    dest2 = offs[e2] + route[:, 5].astype(jnp.int32)
    blk_start = jnp.arange(n_slots // tb, dtype=jnp.int32) * tb
    block_expert = jnp.minimum(jnp.sum(blk_start[:, None] >= ends[None, :], axis=1), N_EXPERTS - 1)
    n_valid = (ends[-1:] // tb).astype(jnp.int32)
    rows_valid = jnp.clip((offs + counts)[block_expert] - blk_start, 0, tb).astype(jnp.int32)
    rows_valid = jnp.where(blk_start < ends[-1], rows_valid, 0)
    xg = _dispatch_call(h, dest1, dest2, n_slots)
    slab = (block_expert + li * N_EXPERTS).astype(jnp.int32)
    y = _grouped_ffn_call(xg, w1, w3, w2, slab, n_valid, rows_valid)
    return _combine_call(y, dest1, dest2, route, xs, g3, mod, n_rows, n_lat)


def _ffn_kernel(x_ref, g2_ref, mod_ref, w1_ref, w3_ref, w2_ref, g3_ref, o_ref, h_ref, acc_ref,
                *, tm, n_rows, n_lat, n_k):
    i = pl.program_id(0)
    k = pl.program_id(1)

    @pl.when(k == 0)
    def _():
        def fill(r0, r1, is_ctx):
            r = 3 if is_ctx else 0
            h = _rms(x_ref[r0:r1, :], g2_ref[...]) * (1.0 + mod_ref[r:r + 1, :]) + mod_ref[r + 1:r + 2, :]
            h_ref[r0:r1, :] = h.astype(BF16)
        _row_groups(i, tm, n_rows, n_lat, fill)
        acc_ref[...] = jnp.zeros_like(acc_ref)

    h = h_ref[...]
    u1 = jnp.dot(h, w1_ref[0], preferred_element_type=F32)
    u3 = jnp.dot(h, w3_ref[0], preferred_element_type=F32)
    u = (u1 * _sigmoid(u1)) * u3
    acc_ref[...] += jnp.dot(u.astype(BF16), w2_ref[0], preferred_element_type=F32)

    @pl.when(k == n_k - 1)
    def _():
        def fin(r0, r1, is_ctx):
            r = 5 if is_ctx else 2
            fn = _rms(acc_ref[r0:r1, :], g3_ref[...])
            o_ref[r0:r1, :] = x_ref[r0:r1, :] + mod_ref[r:r + 1, :] * fn
        _row_groups(i, tm, n_rows, n_lat, fin)


def _ffn_call(xs, g2, mod, w1, w3, w2, li, g3, n_rows, n_lat):
    d = xs.shape[1]
    f = w1.shape[2]
    tm = _pick(n_rows, (768, 512, 256))
    tf = _pick(f, (512, 256, 128))
    n_k = f // tf
    kern = functools.partial(_ffn_kernel, tm=tm, n_rows=n_rows, n_lat=n_lat, n_k=n_k)
    return pl.pallas_call(
        kern,
        grid=(n_rows // tm, n_k),
        in_specs=[pl.BlockSpec((tm, d), lambda i, k: (i, 0)),
                  pl.BlockSpec((1, d), lambda i, k: (0, 0)),
                  pl.BlockSpec((8, d), lambda i, k: (0, 0)),
                  pl.BlockSpec((1, d, tf), lambda i, k: (li, 0, k)),
                  pl.BlockSpec((1, d, tf), lambda i, k: (li, 0, k)),
                  pl.BlockSpec((1, tf, d), lambda i, k: (li, k, 0)),
                  pl.BlockSpec((1, d), lambda i, k: (0, 0))],
        out_specs=pl.BlockSpec((tm, d), lambda i, k: (i, 0)),
        out_shape=jax.ShapeDtypeStruct((n_rows, d), F32),
        scratch_shapes=[pltpu.VMEM((tm, d), BF16), pltpu.VMEM((tm, d), F32)],
        compiler_params=_cparams(("arbitrary", "arbitrary")),
        name="dense_ffn",
    )(xs, g2.reshape(1, d), mod, w1, w3, w2, g3.reshape(1, d))


def _w_in_layout_kernel(a_ref, b_ref, o_ref, *, n_head, n_body):
    j = pl.program_id(1)
    tr, sh = o_ref.shape[1], b_ref.shape[1]

    @pl.when(j < n_head)
    def _():
        o_ref[0] = a_ref[0].astype(BF16)

    @pl.when(jnp.logical_and(j >= n_head, j < n_head + n_body))
    def _():
        o_ref[0, 0:tr - sh, :] = a_ref[0, sh:tr, :].astype(BF16)
        o_ref[0, tr - sh:tr, :] = b_ref[0].astype(BF16)

    @pl.when(j == n_head + n_body)
    def _():
        o_ref[0, 0:sh, :] = b_ref[0].astype(BF16)
        o_ref[0, sh:tr, :] = jnp.zeros((tr - sh, o_ref.shape[2]), BF16)


def _prep_w_in(w_in):
    depth, d, n_in = w_in.shape
    tr, sh = 256, MLA_ROPE
    o_kr = MLA_Q_RANK + MLA_KV_RANK
    assert n_in - sh == _OFF_KR and o_kr % tr == 0 and _OFF_KR % tr == 0 and D_IN_PAD == _OFF_KR + tr
    n_head, n_body = o_kr // tr, (_OFF_KR - o_kr) // tr
    last_a = n_head + n_body - 1
    kern = functools.partial(_w_in_layout_kernel, n_head=n_head, n_body=n_body)
    wt = jnp.swapaxes(w_in, 1, 2)
    return pl.pallas_call(
        kern,
        grid=(depth, n_head + n_body + 1),
        in_specs=[pl.BlockSpec((1, tr, d), lambda l, j: (l, jnp.minimum(j, last_a), 0)),
                  pl.BlockSpec((1, sh, d), lambda l, j: (l, jnp.where(j > last_a, o_kr // sh, (j + 1) * (tr // sh)), 0))],
        out_specs=pl.BlockSpec((1, tr, d), lambda l, j: (l, j, 0)),
        out_shape=jax.ShapeDtypeStruct((depth, D_IN_PAD, d), BF16),
        compiler_params=_cparams(("arbitrary", "arbitrary")),
        name="w_in_layout",
    )(wt, wt)


def _prep_w_uq(w_uq):
    lead = w_uq.shape[:-1]
    w = w_uq.reshape(lead + (MLA_HEADS, MLA_NOPE + MLA_ROPE))
    w = jnp.pad(w, [(0, 0)] * len(lead) + [(0, 0), (0, MLA_QK_PAD - MLA_NOPE - MLA_ROPE)])
    return w.reshape(lead + (MLA_HEADS * MLA_QK_PAD,)).astype(BF16)


def _prep_w_ukv(w_ukv):
    lead = w_ukv.shape[:-1]
    w = w_ukv.reshape(lead + (MLA_HEADS, MLA_NOPE + MLA_V))
    wk = w[..., :MLA_NOPE].reshape(lead + (MLA_HEADS * MLA_NOPE,))
    wv = w[..., MLA_NOPE:].reshape(lead + (MLA_HEADS * MLA_V,))
    return wk.astype(BF16), wv.astype(BF16)


def _rope_tables(n_lat, n_ctx):
    n_grid_rows = n_lat // GRID_W

    def table(half):
        freqs = ROPE_BASE ** (-jnp.arange(half, dtype=F32) / half)
        ar = jnp.arange(n_grid_rows, dtype=jnp.int32).astype(F32)[:, None] * freqs
        ac = jnp.arange(GRID_W, dtype=jnp.int32).astype(F32)[:, None] * freqs
        cos_r, sin_r = jnp.repeat(jnp.cos(ar), GRID_W, axis=0), jnp.repeat(jnp.sin(ar), GRID_W, axis=0)
        cos_c, sin_c = jnp.tile(jnp.cos(ac), (n_grid_rows, 1)), jnp.tile(jnp.sin(ac), (n_grid_rows, 1))
        z = jnp.zeros((n_lat, half), F32)
        pad = jnp.zeros((n_lat, LANES - 4 * half), F32)
        cos = jnp.concatenate([cos_r, cos_r, cos_c, cos_c, 1.0 + pad], axis=-1)
        sin_lo = jnp.concatenate([-sin_r, z, -sin_c, z, pad], axis=-1)
        sin_hi = jnp.concatenate([z, sin_r, z, sin_c, pad], axis=-1)
        ident = jnp.ones((n_ctx, LANES), F32), jnp.zeros((n_ctx, LANES), F32), jnp.zeros((n_ctx, LANES), F32)
        return tuple(jnp.concatenate([a, b], axis=0) for a, b in zip((cos, sin_lo, sin_hi), ident))

    return table(DIFF_QK // 4) + table(MLA_ROPE // 4)


def kernel(x, c, ctx, c_ctx, w_mod, b_mod, g_norm, w_in, w_uq, g_qn, w_ukv, g_kvn, lam, g_sub, w_po_mla,
           w_po_diff, w_out, w1_dense, w3_dense, w2_dense, w_router, w1_moe, w3_moe, w2_moe):
    b, s, d = x.shape
    n_ctx = ctx.shape[1]
    depth = w_mod.shape[0]
    assert b == 1 and d == D_MODEL and s % GRID_W == 0
    n_tok = s + n_ctx

    xs = jnp.concatenate([x[0], ctx[0]], axis=0)
    c8 = jnp.concatenate([c, c_ctx[None, :], jnp.zeros((6, d), F32)], axis=0)
    mods = _mod_call(c8, w_mod, b_mod)

    w_in_p = _prep_w_in(w_in)
    w_uq_p = _prep_w_uq(w_uq)
    w_k_p, w_v_p = _prep_w_ukv(w_ukv)
    w_pm, w_pd, w_o = w_po_mla.astype(BF16), w_po_diff.astype(BF16), w_out.astype(BF16)
    w1d, w3d, w2d = w1_dense.astype(BF16), w3_dense.astype(BF16), w2_dense.astype(BF16)
    wr_pad = jnp.pad(w_router, ((0, 0), (0, 0), (0, LANES - N_EXPERTS)))
    flat = lambda w: w.reshape((-1,) + w.shape[2:])
    w1m, w3m, w2m = flat(w1_moe), flat(w3_moe), flat(w2_moe)
    tabs = _rope_tables(s, n_ctx)
    zrow = jnp.zeros((d,), F32)

    for layer in range(depth):
        need_ctx = layer < depth - 1
        lat, cx = mods[layer, 0], mods[layer, 1]
        seg = lambda v, k: v[k * d:(k + 1) * d]
        gn = g_norm[layer]

        mod1 = jnp.stack([seg(lat, 1), seg(lat, 0), seg(cx, 1), seg(cx, 0), zrow, zrow, zrow, zrow])
        proj = _norm_matmul_call(xs, gn[0], mod1, w_in_p, layer, s)
        qm, km, vm, qd, kd = _prep_call(proj, g_qn[layer].reshape(1, -1), g_kvn[layer].reshape(1, -1),
                                        w_uq_p, w_k_p, w_v_p, layer, tabs)
        lam_init = 0.8 - 0.6 * math.exp(-0.3 * layer)
        o_mla = _mla_attn_call(qm, km, vm, 0, s, 0, n_tok)
        o_d = _diff_attn_call(qd, kd, proj, lam[layer], g_sub[layer], lam_init, 0, s, 0, n_tok)
        o_mla_c = o_d_c = None
        if need_ctx:
            o_mla_c = _mla_attn_call(qm, km, vm, s, n_ctx, s, n_ctx)
            o_d_c = _diff_attn_call(qd, kd, proj, lam[layer], g_sub[layer], lam_init, s, n_ctx, s, n_ctx)
        n_rows = n_tok if need_ctx else s
        gate1 = jnp.stack([seg(lat, 2), seg(cx, 2), zrow, zrow, zrow, zrow, zrow, zrow])
        xs = _post_call(o_mla, o_d, o_mla_c, o_d_c, proj, xs, w_pm, w_pd, w_o, layer, gn[1], gate1, s)

        mod2 = jnp.stack([seg(lat, 4), seg(lat, 3), seg(lat, 5), seg(cx, 4), seg(cx, 3), seg(cx, 5), zrow, zrow])
        i = layer // 2
        if layer % 2 == 0:
            xs = _ffn_call(xs, gn[2], mod2, w1d, w3d, w2d, i, gn[3], n_rows, s)
        else:
            xs = _moe_call(xs, gn[2], mod2, wr_pad[i], w1m, w3m, w2m, i, gn[3], n_rows, s)
    return xs[:s][None]
```
